```python
import jax, jax.numpy as jnp
from jax import lax
import numpy as np

D_MODEL = 1024
BATCH = 8
SEQ = 2048
DEPTH = 4
DEC_BATCH = 128
DEC_SEQ = 1
PAST_LEN = 16384
PAGE_SIZE = 128

N_MEM = 256
HEAD_DIM = 64
POOL_WIDTH = D_MODEL // 4
POOL_WINDOWS = (2, 4, 8, 16)
POOL_GROUP = POOL_WIDTH // len(POOL_WINDOWS)
POOL_BUF = max(POOL_WINDOWS) - 1
RWKV_WIDTH = D_MODEL // 2
RWKV_HEADS = RWKV_WIDTH // HEAD_DIM
XA_WIDTH = D_MODEL // 4
XA_HEADS = 4
XA_HEAD_DIM = XA_WIDTH // XA_HEADS
MIX_WIDTH = POOL_WIDTH + RWKV_WIDTH + XA_WIDTH
DECAY_LORA = 64
ICLR_LORA = 64
SHIFT_WIDTH = 3 * RWKV_WIDTH + DECAY_LORA + ICLR_LORA
IN_WIDTH = 2 * POOL_WIDTH + SHIFT_WIDTH + RWKV_WIDTH + 2 * XA_WIDTH
IN_SPLITS = [POOL_WIDTH, 2 * POOL_WIDTH, 2 * POOL_WIDTH + SHIFT_WIDTH,
             2 * POOL_WIDTH + SHIFT_WIDTH + RWKV_WIDTH,
             2 * POOL_WIDTH + SHIFT_WIDTH + RWKV_WIDTH + XA_WIDTH]
EPS = 1e-6
GN_EPS = HEAD_DIM * 1e-5

kernel_name = 'pool_rwkv7_memxattn_hybrid_step'

F32 = jnp.float32


def _rms_norm(x, g):
    xf = x.astype(F32)
    y = xf * lax.rsqrt(jnp.mean(xf * xf, axis=-1, keepdims=True) + EPS)
    return (y * g.astype(F32)).astype(x.dtype)


def _pool_mix(v, buf, start_pos, pool_w, pool_scale):
    B, T, _ = v.shape
    full = jnp.concatenate([buf, v], axis=1).astype(F32)
    csum = jnp.concatenate([jnp.zeros((B, 1, POOL_WIDTH), F32), jnp.cumsum(full, axis=1)], axis=1)
    pos = start_pos + jnp.arange(T)
    cur = full[:, POOL_BUF:]
    hi = csum[:, POOL_BUF + 1:POOL_BUF + 1 + T]
    groups = []
    for gi, win in enumerate(POOL_WINDOWS):
        sl = slice(gi * POOL_GROUP, (gi + 1) * POOL_GROUP)
        lo = csum[:, POOL_BUF + 1 - win:POOL_BUF + 1 - win + T, sl]
        cnt = jnp.minimum(pos + 1, win).astype(F32)[None, :, None]
        groups.append((hi[..., sl] - lo) / cnt - cur[..., sl])
    pooled = jnp.stack(groups, axis=2)
    y = jnp.einsum('btng,nge->btne', pooled, pool_w.astype(F32)).reshape(B, T, POOL_WIDTH)
    return (y * pool_scale.astype(F32)).astype(v.dtype)


def _rwkv7(xs, S0, w0, w_w2, a0, w_a2, k_k, k_a, r_k, ln_g, ln_b):
    B, T, _ = xs.shape
    H, N, R = RWKV_HEADS, HEAD_DIM, RWKV_WIDTH
    xs = xs.astype(F32)
    r = xs[..., :R]
    k = xs[..., R:2 * R]
    v = xs[..., 2 * R:3 * R]
    wd = xs[..., 3 * R:3 * R + DECAY_LORA]
    ad = xs[..., 3 * R + DECAY_LORA:]
    w = -jax.nn.softplus(-(w0.astype(F32) + jnp.tanh(wd) @ w_w2.astype(F32))) - 0.5
    decay = jnp.exp(-jnp.exp(w))
    a = jax.nn.sigmoid(a0.astype(F32) + ad @ w_a2.astype(F32))
    heads = lambda t: t.reshape(B, T, H, N)
    kk = heads(k * k_k.astype(F32))
    kk = kk / jnp.maximum(jnp.sqrt(jnp.sum(kk * kk, axis=-1, keepdims=True)), 1e-12)
    k = k * (1.0 + (a - 1.0) * k_a.astype(F32))
    r, k, v, decay, a = heads(r), heads(k), heads(v), heads(decay), heads(a)

    def step(S, inp):
        r_t, w_t, k_t, v_t, kk_t, a_t = inp
        s_kk = jnp.einsum('bhij,bhj->bhi', S, kk_t)
        S = (S * w_t[:, :, None, :] - s_kk[..., None] * (kk_t * a_t)[:, :, None, :]
             + v_t[..., None] * k_t[:, :, None, :])
        return S, jnp.einsum('bhij,bhj->bhi', S, r_t)

    tm = lambda t: jnp.swapaxes(t, 0, 1)
    S_T, y = lax.scan(step, S0.astype(F32), (tm(r), tm(decay), tm(k), tm(v), tm(kk), tm(a)))
    y = tm(y)
    mu = jnp.mean(y, axis=-1, keepdims=True)
    var = jnp.mean(jnp.square(y - mu), axis=-1, keepdims=True)
    y = (y - mu) * lax.rsqrt(var + GN_EPS) * ln_g.astype(F32).reshape(H, N) + ln_b.astype(F32).reshape(H, N)
    y = y + jnp.sum(r * k * r_k.astype(F32), axis=-1, keepdims=True) * v
    return y.reshape(B, T, R), S_T


def _mem_kv(mem, g, w_kv):
    B = mem.shape[0]
    kv = _rms_norm(mem, g) @ w_kv
    k = kv[..., :XA_WIDTH].reshape(B, N_MEM, XA_HEADS, XA_HEAD_DIM)
    v = kv[..., XA_WIDTH:].reshape(B, N_MEM, XA_HEADS, XA_HEAD_DIM)
    return k, v


def _cross_attend(q, mk, mv):
    B, T, _ = q.shape
    qh = q.reshape(B, T, XA_HEADS, XA_HEAD_DIM).astype(F32)
    s = jnp.einsum('bthd,bmhd->bhtm', qh, mk.astype(F32)) * (XA_HEAD_DIM ** -0.5)
    p = jax.nn.softmax(s, axis=-1)
    o = jnp.einsum('bhtm,bmhd->bthd', p, mv.astype(F32))
    return o.reshape(B, T, XA_WIDTH).astype(q.dtype)


def _trunk(x, start_pos, pool_buf, shift_prev, wkv, mem_k, mem_v, p):
    new_pool, new_shift, new_wkv = [], [], []
    for l in range(DEPTH):
        xn = _rms_norm(x, p['norm_g'][l])
        h = xn @ p['w_in'][l]
        pool_v, pool_g, rw, rw_g, q, xa_g = jnp.split(h, IN_SPLITS, axis=-1)
        pool_y = _pool_mix(pool_v, pool_buf[l], start_pos, p['pool_w'][l], p['pool_scale'][l]) * jax.nn.silu(pool_g)
        new_pool.append(jnp.concatenate([pool_buf[l].astype(pool_v.dtype), pool_v], axis=1)[:, -POOL_BUF:])
        prev = jnp.concatenate([shift_prev[l][:, None].astype(rw.dtype), rw[:, :-1]], axis=1)
        rws = rw + (prev - rw) * p['shift_mu'][l]
        new_shift.append(rw[:, -1])
        rwkv_y, S_T = _rwkv7(rws, wkv[l], p['w0'][l], p['w_w2'][l], p['a0'][l], p['w_a2'][l],
                             p['k_k'][l], p['k_a'][l], p['r_k'][l], p['ln_x_g'][l], p['ln_x_b'][l])
        new_wkv.append(S_T.astype(x.dtype))
        rwkv_y = rwkv_y.astype(x.dtype) * jax.nn.silu(rw_g)
        xa_y = _cross_attend(q, mem_k[l], mem_v[l]) * jax.nn.silu(xa_g)
        mixed = jnp.concatenate([pool_y, rwkv_y, xa_y], axis=-1)
        x = x + mixed @ p['w_out'][l]
    y = _rms_norm(x, p['final_norm_g'])
    return y, jnp.stack(new_pool), jnp.stack(new_shift), jnp.stack(new_wkv)


def setup_inputs(seed: int = 0) -> dict:
    key = jax.random.key(seed)
    ks = jax.random.split(key, 32)
    nrm = lambda k, shape, s: jax.random.normal(k, shape, F32) * s
    L = DEPTH
    return {
        'x_prompt': nrm(ks[0], (BATCH, SEQ, D_MODEL), 1.0),
        'x_sample': nrm(ks[1], (DEC_BATCH, DEC_SEQ, D_MODEL), 1.0),
        'mem_prompt': nrm(ks[2], (BATCH, N_MEM, D_MODEL), 1.0),
        'state_pool': nrm(ks[3], (L, DEC_BATCH, POOL_BUF, POOL_WIDTH), 1.0),
        'state_shift': nrm(ks[4], (L, DEC_BATCH, SHIFT_WIDTH), 1.0),
        'state_wkv': nrm(ks[5], (L, DEC_BATCH, RWKV_HEADS, HEAD_DIM, HEAD_DIM), 0.5),
        'cache_mem_k': nrm(ks[6], (L, DEC_BATCH, N_MEM, XA_HEADS, XA_HEAD_DIM), 1.0),
        'cache_mem_v': nrm(ks[7], (L, DEC_BATCH, N_MEM, XA_HEADS, XA_HEAD_DIM), 1.0),
        'norm_g': 1.0 + nrm(ks[8], (L, D_MODEL), 0.05),
        'w_in': nrm(ks[9], (L, D_MODEL, IN_WIDTH), D_MODEL ** -0.5),
        'w_out': nrm(ks[10], (L, MIX_WIDTH, D_MODEL), 0.5 * MIX_WIDTH ** -0.5),
        'pool_w': nrm(ks[11], (L, len(POOL_WINDOWS), POOL_GROUP, POOL_GROUP), POOL_GROUP ** -0.5),
        'pool_scale': 1.0 + nrm(ks[12], (L, POOL_WIDTH), 0.1),
        'shift_mu': jax.random.uniform(ks[13], (L, SHIFT_WIDTH), F32),
        'w0': -1.0 + nrm(ks[14], (L, RWKV_WIDTH), 0.5),
        'w_w2': nrm(ks[15], (L, DECAY_LORA, RWKV_WIDTH), 0.1),
        'a0': nrm(ks[16], (L, RWKV_WIDTH), 0.1),
        'w_a2': nrm(ks[17], (L, ICLR_LORA, RWKV_WIDTH), 0.1),
        'k_k': 0.85 + nrm(ks[18], (L, RWKV_WIDTH), 0.05),
        'k_a': 1.0 + nrm(ks[19], (L, RWKV_WIDTH), 0.05),
        'r_k': nrm(ks[20], (L, RWKV_HEADS, HEAD_DIM), 0.1),
        'ln_x_g': 1.0 + nrm(ks[21], (L, RWKV_WIDTH), 0.1),
        'ln_x_b': nrm(ks[22], (L, RWKV_WIDTH), 0.01),
        'mem_norm_g': 1.0 + nrm(ks[23], (L, D_MODEL), 0.05),
        'w_kv': nrm(ks[24], (L, D_MODEL, 2 * XA_WIDTH), D_MODEL ** -0.5),
        'final_norm_g': 1.0 + nrm(ks[25], (D_MODEL,), 0.05),
    }


def reference(x_prompt, x_sample, mem_prompt, state_pool, state_shift, state_wkv, cache_mem_k, cache_mem_v,
              norm_g, w_in, w_out, pool_w, pool_scale, shift_mu, w0, w_w2, a0, w_a2, k_k, k_a, r_k,
              ln_x_g, ln_x_b, mem_norm_g, w_kv, final_norm_g):
    params = {'norm_g': norm_g, 'w_in': w_in, 'w_out': w_out, 'pool_w': pool_w, 'pool_scale': pool_scale,
              'shift_mu': shift_mu, 'w0': w0, 'w_w2': w_w2, 'a0': a0, 'w_a2': w_a2, 'k_k': k_k, 'k_a': k_a,
              'r_k': r_k, 'ln_x_g': ln_x_g, 'ln_x_b': ln_x_b, 'final_norm_g': final_norm_g}
    mk_list, mv_list = [], []
    for l in range(DEPTH):
        mk, mv = _mem_kv(mem_prompt, mem_norm_g[l], w_kv[l])
        mk_list.append(mk)
        mv_list.append(mv)
    memk_prompt = jnp.stack(mk_list)
    memv_prompt = jnp.stack(mv_list)
    dt = x_prompt.dtype
    pool0 = jnp.zeros((DEPTH, BATCH, POOL_BUF, POOL_WIDTH), dt)
    shift0 = jnp.zeros((DEPTH, BATCH, SHIFT_WIDTH), dt)
    wkv0 = jnp.zeros((DEPTH, BATCH, RWKV_HEADS, HEAD_DIM, HEAD_DIM), F32)
    y_prompt, pool_prompt, shift_prompt, wkv_prompt = _trunk(
        x_prompt, 0, pool0, shift0, wkv0, memk_prompt, memv_prompt, params)
    y_sample, pool_sample, shift_sample, wkv_sample = _trunk(
        x_sample, PAST_LEN, state_pool, state_shift, state_wkv, cache_mem_k, cache_mem_v, params)
    return (y_prompt, y_sample, pool_prompt, shift_prompt, wkv_prompt, memk_prompt, memv_prompt,
            pool_sample, shift_sample, wkv_sample)
```

```python
import functools

import jax
import jax.numpy as jnp
from jax import lax
from jax.experimental import pallas as pl
from jax.experimental.pallas import tpu as pltpu

F32 = jnp.float32
BF16 = jnp.bfloat16
HIGHEST = lax.Precision.HIGHEST

HEAD_DIM = 64
POOL_WINDOWS = (2, 4, 8, 16)
POOL_BUF = max(POOL_WINDOWS) - 1
POOL_HALO = 16
XA_HEADS = 4
DECAY_LORA = 64
ICLR_LORA = 64
EPS = 1e-6
GN_EPS = HEAD_DIM * 1e-5
LANES = 128
VMEM_LIMIT = 48 * 1024 * 1024


def _cparams(sem):
    return pltpu.CompilerParams(dimension_semantics=sem, vmem_limit_bytes=VMEM_LIMIT)


def _silu(x):
    return x / (1.0 + jnp.exp(-x))


def _sigmoid(x):
    return 1.0 / (1.0 + jnp.exp(-x))


def _softplus(x):
    return jnp.maximum(x, 0.0) + jnp.log1p(jnp.exp(-jnp.abs(x)))


def _rms(x, g):
    return x * lax.rsqrt(jnp.mean(x * x, axis=-1, keepdims=True) + EPS) * g


def _proj_kernel(x_ref, g_ref, w_ref, *out_refs, splits):
    xn = _rms(x_ref[...], g_ref[...]).astype(BF16)
    off = 0
    for o_ref, n in zip(out_refs, splits):
        o_ref[...] = jnp.dot(xn, w_ref[:, off:off + n], preferred_element_type=F32)
        off += n


def _project(x2d, g, w_bf16, splits, tm):
    m, d = x2d.shape
    n = w_bf16.shape[1]
    return pl.pallas_call(
        functools.partial(_proj_kernel, splits=splits),
        grid=(m // tm,),
        in_specs=[pl.BlockSpec((tm, d), lambda i: (i, 0)),
                  pl.BlockSpec((1, d), lambda i: (0, 0)),
                  pl.BlockSpec((d, n), lambda i: (0, 0))],
        out_specs=[pl.BlockSpec((tm, s), lambda i: (i, 0)) for s in splits],
        out_shape=[jax.ShapeDtypeStruct((m, s), F32) for s in splits],
        compiler_params=_cparams(("arbitrary",)),
        name="proj",
    )(x2d, g.reshape(1, d), w_bf16)


def _rwkv_vectors(rws, w0, w_w2, a0, w_a2, k_k, k_a, width):
    r = rws[:, :width]
    k = rws[:, width:2 * width]
    v = rws[:, 2 * width:3 * width]
    wd = rws[:, 3 * width:3 * width + DECAY_LORA]
    ad = rws[:, 3 * width + DECAY_LORA:]
    wlin = w0 + jnp.dot(jnp.tanh(wd), w_w2, precision=HIGHEST, preferred_element_type=F32)
    w = -_softplus(-wlin) - 0.5
    decay = jnp.exp(-jnp.exp(w))
    a = _sigmoid(a0 + jnp.dot(ad, w_a2, precision=HIGHEST, preferred_element_type=F32))
    kk_raw = k * k_k
    k = k * (1.0 + (a - 1.0) * k_a)
    return r, k, v, decay, kk_raw, a


def _prep_kernel(rw_ref, mu_ref, w0_ref, ww2_ref, a0_ref, wa2_ref, kk_ref, ka_ref,
                 r_o, k_o, v_o, d_o, kk_o, b_o, carry, *, tt, width):
    @pl.when(pl.program_id(1) == 0)
    def _():
        carry[...] = jnp.zeros_like(carry)

    rw = rw_ref[...]
    row = lax.broadcasted_iota(jnp.int32, rw.shape, 0)
    prev = jnp.where(row == 0, carry[0:1, :], pltpu.roll(rw, 1, 0))
    carry[0:1, :] = rw[tt - 1:tt, :]
    rws = rw + (prev - rw) * mu_ref[...]
    r, k, v, decay, kk_raw, a = _rwkv_vectors(
        rws, w0_ref[...], ww2_ref[...], a0_ref[...], wa2_ref[...], kk_ref[...], ka_ref[...], width)
    r_o[...] = r
    k_o[...] = k
    v_o[...] = v
    d_o[...] = decay
    kk_o[...] = kk_raw
    b_o[...] = kk_raw * a


def _rwkv_prep(rw, mu, w0, w_w2, a0, w_a2, k_k, k_a, tt):
    b, t, sw = rw.shape
    width = w0.shape[0]
    row = lambda a: a.reshape(1, -1)
    const = lambda shape: pl.BlockSpec(shape, lambda bi, ti: (0,) * len(shape))
    tile = lambda c: pl.BlockSpec((None, tt, c), lambda bi, ti: (bi, ti, 0))
    return pl.pallas_call(
        functools.partial(_prep_kernel, tt=tt, width=width),
        grid=(b, t // tt),
        in_specs=[tile(sw), const((1, sw)), const((1, width)), const(w_w2.shape), const((1, width)),
                  const(w_a2.shape), const((1, width)), const((1, width))],
        out_specs=[tile(width)] * 6,
        out_shape=[jax.ShapeDtypeStruct((b, t, width), F32)] * 6,
        scratch_shapes=[pltpu.VMEM((8, sw), F32)],
        compiler_params=_cparams(("arbitrary", "arbitrary")),
        name="rwkv_prep",
    )(rw, row(mu), row(w0), w_w2, row(a0), w_a2, row(k_k), row(k_a))


def _scan_kernel(w_ref, b_ref, k_ref, r_ref, kk_ref, v_ref, rk_ref, g_ref, be_ref,
                 y_ref, s_ref, u_scr, *, tt, n):
    half = n // 2

    def inv_norm_sq(t):
        kk = kk_ref[t]
        nrm = jnp.maximum(jnp.sqrt(jnp.sum(kk * kk, axis=0, keepdims=True)), 1e-12)
        inv = 1.0 / nrm
        return inv * inv

    @pl.when(pl.program_id(0) == 0)
    def _():
        s_ref[...] = jnp.zeros_like(s_ref)

    acc = jnp.zeros((half, LANES), F32)
    for j in range(n):
        acc = acc + s_ref[j] * kk_ref[0, j:j + 1, :]
    u_scr[...] = acc * inv_norm_sq(0)

    def step(t, carry):
        tn = jnp.minimum(t + 1, tt - 1)
        u = u_scr[...]
        v = v_ref[t]
        y = jnp.zeros((half, LANES), F32)
        nacc = jnp.zeros((half, LANES), F32)
        for j in range(n):
            s = s_ref[j] * w_ref[t, j:j + 1, :] - u * b_ref[t, j:j + 1, :] + v * k_ref[t, j:j + 1, :]
            s_ref[j] = s
            y = y + s * r_ref[t, j:j + 1, :]
            nacc = nacc + s * kk_ref[tn, j:j + 1, :]
        u_scr[...] = nacc * inv_norm_sq(tn)

        def head_mean(x):
            m = jnp.broadcast_to(jnp.sum(x, axis=0, keepdims=True), (8, LANES))
            return (m + pltpu.roll(m, LANES // 2, 1))[0:1, :] * (1.0 / n)

        d = y - head_mean(y)
        var = head_mean(d * d)
        bonus = jnp.sum(r_ref[t] * k_ref[t] * rk_ref[...], axis=0, keepdims=True)
        y_ref[t] = d * lax.rsqrt(var + GN_EPS) * g_ref[...] + be_ref[...] + bonus * v
        return carry

    lax.fori_loop(0, tt, step, 0)


def _rwkv_scan(w_c, b_c, k_c, r_c, kk_c, v_c, rk_c, g_c, be_c, tt):
    t, n, _ = w_c.shape
    half = n // 2
    jop = pl.BlockSpec((tt, n, LANES), lambda ti: (ti, 0, 0))
    iop = pl.BlockSpec((tt, half, LANES), lambda ti: (ti, 0, 0))
    const = lambda shape: pl.BlockSpec(shape, lambda ti: (0,) * len(shape))
    return pl.pallas_call(
        functools.partial(_scan_kernel, tt=tt, n=n),
        grid=(t // tt,),
        in_specs=[jop] * 5 + [iop, const((n, LANES)), const((half, LANES)), const((half, LANES))],
        out_specs=[iop, const((n, half, LANES))],
        out_shape=[jax.ShapeDtypeStruct((t, half, LANES), F32),
                   jax.ShapeDtypeStruct((n, half, LANES), F32)],
        scratch_shapes=[pltpu.VMEM((half, LANES), F32)],
        compiler_params=_cparams(("arbitrary",)),
        name="rwkv_scan",
    )(w_c, b_c, k_c, r_c, kk_c, v_c, rk_c, g_c, be_c)


def _to_chain_j(x, heads):
    b, t, _ = x.shape
    xc = x.reshape(b, t, heads, HEAD_DIM).transpose(1, 3, 0, 2).reshape(t, HEAD_DIM, b * heads)
    return jnp.concatenate([xc, xc], axis=-1)


def _to_chain_i(x, heads):
    b, t, _ = x.shape
    half = HEAD_DIM // 2
    xc = x.reshape(b, t, heads, 2, half).transpose(1, 4, 3, 0, 2)
    return xc.reshape(t, half, 2 * b * heads)


def _from_chain_i(y_c, b, heads):
    t, half, _ = y_c.shape
    return y_c.reshape(t, half, 2, b, heads).transpose(3, 0, 4, 2, 1).reshape(b, t, heads * HEAD_DIM)


def _pool_select(lane_group, vals):
    out = vals[-1]
    for gi in range(len(vals) - 2, -1, -1):
        out = jnp.where(lane_group == gi, vals[gi], out)
    return out


def _cross_attend_rows(qm, mk, mv):
    hd = mk.shape[1] // XA_HEADS
    s = lax.dot_general(qm, mk, (((1,), (1,)), ((), ())), preferred_element_type=F32) * (hd ** -0.5)
    p = jnp.exp(s - jnp.max(s, axis=-1, keepdims=True))
    p = p / jnp.sum(p, axis=-1, keepdims=True)
    return jnp.dot(p.astype(mv.dtype), mv, preferred_element_type=F32)


def _mix_kernel(x_ref, pv_ref, pg_ref, yr_ref, rg_ref, q_ref, xg_ref, mk_ref, mv_ref,
                wbd_ref, ps_ref, wo_ref, fg_ref, o_ref, ext, *, tt, final):
    ti = pl.program_id(1)

    @pl.when(ti == 0)
    def _():
        ext[0:POOL_HALO, :] = jnp.zeros((POOL_HALO, ext.shape[1]), F32)

    pv = pv_ref[...]
    pw = pv.shape[1]
    ext[POOL_HALO:POOL_HALO + tt, :] = pv
    sums = []
    acc = pv
    for back in range(1, max(POOL_WINDOWS)):
        acc = acc + ext[pl.ds(POOL_HALO - back, tt), :]
        if back + 1 in POOL_WINDOWS:
            sums.append(acc)
    tail = ext[tt:tt + POOL_HALO, :]
    ext[0:POOL_HALO, :] = tail
    group = lax.broadcasted_iota(jnp.int32, (tt, pw), 1) // (pw // len(POOL_WINDOWS))
    pos = ti * tt + lax.broadcasted_iota(jnp.int32, (tt, pw), 0)
    win = _pool_select(group, [jnp.full((tt, pw), w, jnp.int32) for w in POOL_WINDOWS])
    cnt = jnp.minimum(pos + 1, win).astype(F32)
    pooled = _pool_select(group, sums) / cnt - pv
    pool_y = jnp.dot(pooled, wbd_ref[...], precision=HIGHEST, preferred_element_type=F32)
    pool_y = pool_y * ps_ref[...] * _silu(pg_ref[...])

    q = q_ref[...]
    xw = q.shape[1]
    mk = mk_ref[...].astype(BF16)
    mv = mv_ref[...].astype(BF16)
    head = lax.broadcasted_iota(jnp.int32, (tt, xw), 1) // (xw // XA_HEADS)
    xa = jnp.zeros((tt, xw), F32)
    for h in range(XA_HEADS):
        qm = jnp.where(head == h, q, 0.0).astype(BF16)
        xa = jnp.where(head == h, _cross_attend_rows(qm, mk, mv), xa)
    xa_y = xa * _silu(xg_ref[...])

    rwkv_y = yr_ref[...] * _silu(rg_ref[...])

    rwid = rwkv_y.shape[1]
    out = x_ref[...]
    out = out + jnp.dot(pool_y.astype(BF16), wo_ref[0:pw, :], preferred_element_type=F32)
    out = out + jnp.dot(rwkv_y.astype(BF16), wo_ref[pw:pw + rwid, :], preferred_element_type=F32)
    out = out + jnp.dot(xa_y.astype(BF16), wo_ref[pw + rwid:, :], preferred_element_type=F32)
    if final:
        out = _rms(out, fg_ref[...])
    o_ref[...] = out


def _prompt_mix(x, pv, pg, yr, rg, q, xg, mk, mv, wbd, pscale, wo_bf16, fin_g, tt, final):
    b, t, d = x.shape
    tile = lambda c: pl.BlockSpec((None, tt, c), lambda bi, ti: (bi, ti, 0))
    perb = lambda a: pl.BlockSpec((None,) + a.shape[1:], lambda bi, ti: (bi, 0, 0))
    const = lambda a: pl.BlockSpec(a.shape, lambda bi, ti: (0,) * a.ndim)
    pscale = pscale.reshape(1, -1)
    fin_g = fin_g.reshape(1, -1)
    return pl.pallas_call(
        functools.partial(_mix_kernel, tt=tt, final=final),
        grid=(b, t // tt),
        in_specs=[tile(d), tile(pv.shape[2]), tile(pg.shape[2]), tile(yr.shape[2]), tile(rg.shape[2]),
                  tile(q.shape[2]), tile(xg.shape[2]), perb(mk), perb(mv),
                  const(wbd), const(pscale), const(wo_bf16), const(fin_g)],
        out_specs=tile(d),
        out_shape=jax.ShapeDtypeStruct((b, t, d), F32),
        scratch_shapes=[pltpu.VMEM((tt + POOL_HALO, pv.shape[2]), F32)],
        compiler_params=_cparams(("arbitrary", "arbitrary")),
        name="prompt_mix",
    )(x, pv, pg, yr, rg, q, xg, mk, mv, wbd, pscale, wo_bf16, fin_g)


def _sample_kernel(pv_ref, pg_ref, rw_ref, rg_ref, q_ref, xg_ref, pbuf_ref, sprev_ref, s_ref,
                   ck_ref, cv_ref, wbd_ref, ps_ref, mu_ref, w0_ref, ww2_ref, a0_ref, wa2_ref,
                   kk_ref, ka_ref, rk_ref, g_ref, be_ref,
                   mixed_ref, so_ref, y_scr, xa_scr, *, bt, start_pos):
    n = HEAD_DIM
    pv = pv_ref[...]
    pw = pv.shape[1]
    sums = []
    acc = pv
    for back in range(1, max(POOL_WINDOWS)):
        acc = acc + pbuf_ref[POOL_BUF - back]
        if back + 1 in POOL_WINDOWS:
            sums.append(acc)
    group = lax.broadcasted_iota(jnp.int32, (bt, pw), 1) // (pw // len(POOL_WINDOWS))
    cnts = [jnp.full((bt, pw), float(min(start_pos + 1, w)), F32) for w in POOL_WINDOWS]
    pooled = _pool_select(group, sums) / _pool_select(group, cnts) - pv
    pool_y = jnp.dot(pooled, wbd_ref[...], precision=HIGHEST, preferred_element_type=F32)
    pool_y = pool_y * ps_ref[...] * _silu(pg_ref[...])

    rw = rw_ref[...]
    rws = rw + (sprev_ref[...] - rw) * mu_ref[...]
    width = w0_ref.shape[1]
    heads = width // n
    r, k, v, decay, kk_raw, a = _rwkv_vectors(
        rws, w0_ref[...], ww2_ref[...], a0_ref[...], wa2_ref[...], kk_ref[...], ka_ref[...], width)
    eye = (lax.broadcasted_iota(jnp.int32, (n, n), 0) == lax.broadcasted_iota(jnp.int32, (n, n), 1))[None]
    for h in range(heads):
        sl = slice(h * n, (h + 1) * n)
        r_h, k_h, v_h = r[:, sl], k[:, sl], v[:, sl]
        kk_h = kk_raw[:, sl]
        kk_h = kk_h / jnp.maximum(jnp.sqrt(jnp.sum(kk_h * kk_h, axis=-1, keepdims=True)), 1e-12)
        b_h = kk_h * a[:, sl]
        s = s_ref[:, h]
        s_kk = jnp.sum(s * kk_h[:, None, :], axis=-1, keepdims=True)
        v_col = jnp.sum(jnp.where(eye, v_h[:, None, :], 0.0), axis=-1, keepdims=True)
        s = s * decay[:, sl][:, None, :] - s_kk * b_h[:, None, :] + v_col * k_h[:, None, :]
        so_ref[:, h] = s
        y_col = jnp.sum(s * r_h[:, None, :], axis=-1, keepdims=True)
        y = jnp.sum(jnp.where(eye, y_col, 0.0), axis=1)
        d = y - jnp.mean(y, axis=-1, keepdims=True)
        var = jnp.mean(d * d, axis=-1, keepdims=True)
        y = d * lax.rsqrt(var + GN_EPS) * g_ref[:, sl] + be_ref[:, sl]
        bonus = jnp.sum(r_h * k_h * rk_ref[:, sl], axis=-1, keepdims=True)
        y_scr[:, sl] = y + bonus * v_h
    rwkv_y = y_scr[...] * _silu(rg_ref[...])

    q = q_ref[...]
    xw = q.shape[1]
    row_h = lax.broadcasted_iota(jnp.int32, (8, xw), 0)
    lane_h = lax.broadcasted_iota(jnp.int32, (8, xw), 1) // (xw // XA_HEADS)
    for bi in range(bt):
        qm = jnp.where(row_h == lane_h, q[bi:bi + 1, :], 0.0)
        o8 = _cross_attend_rows(qm, ck_ref[bi], cv_ref[bi])
        xa_scr[bi:bi + 1, :] = jnp.sum(jnp.where(row_h == lane_h, o8, 0.0), axis=0, keepdims=True)
    xa_y = xa_scr[...] * _silu(xg_ref[...])

    mixed_ref[:, 0:pw] = pool_y
    mixed_ref[:, pw:pw + width] = rwkv_y
    mixed_ref[:, pw + width:] = xa_y


def _sample_mix(pv, pg, rw, rg, q, xg, pbuf_t, sprev, s0, ck, cv, wbd, vecs, ww2, wa2, bt, start_pos):
    b = pv.shape[0]
    row = lambda c: pl.BlockSpec((bt, c), lambda i: (i, 0))
    const = lambda a: pl.BlockSpec(a.shape, lambda i: (0,) * a.ndim)
    lead = lambda a: pl.BlockSpec((bt,) + a.shape[1:], lambda i: (i,) + (0,) * (a.ndim - 1))
    pscale, mu, w0, a0, k_k, k_a, r_k, ln_g, ln_b = [a.reshape(1, -1) for a in vecs]
    mixw = pv.shape[1] + w0.shape[1] + q.shape[1]
    return pl.pallas_call(
        functools.partial(_sample_kernel, bt=bt, start_pos=start_pos),
        grid=(b // bt,),
        in_specs=[row(pv.shape[1]), row(pg.shape[1]), row(rw.shape[1]), row(rg.shape[1]),
                  row(q.shape[1]), row(xg.shape[1]),
                  pl.BlockSpec((POOL_BUF, bt, pv.shape[1]), lambda i: (0, i, 0)),
                  row(sprev.shape[1]), lead(s0), lead(ck), lead(cv),
                  const(wbd), const(pscale), const(mu), const(w0), const(ww2), const(a0), const(wa2),
                  const(k_k), const(k_a), const(r_k), const(ln_g), const(ln_b)],
        out_specs=[row(mixw), lead(s0)],
        out_shape=[jax.ShapeDtypeStruct((b, mixw), F32), jax.ShapeDtypeStruct(s0.shape, F32)],
        scratch_shapes=[pltpu.VMEM((bt, w0.shape[1]), F32), pltpu.VMEM((bt, q.shape[1]), F32)],
        compiler_params=_cparams(("arbitrary",)),
        name="sample_mix",
    )(pv, pg, rw, rg, q, xg, pbuf_t, sprev, s0, ck, cv, wbd, pscale, mu, w0, ww2, a0, wa2,
      k_k, k_a, r_k, ln_g, ln_b)


def _out_kernel(x_ref, m_ref, wo_ref, fg_ref, o_ref, *, final):
    out = x_ref[...] + jnp.dot(m_ref[...].astype(BF16), wo_ref[...], preferred_element_type=F32)
    if final:
        out = _rms(out, fg_ref[...])
    o_ref[...] = out


def _out_project(x2d, mixed, wo_bf16, fin_g, final):
    m, d = x2d.shape
    full = lambda a: pl.BlockSpec(a.shape, lambda i: (0,) * a.ndim)
    fin_g = fin_g.reshape(1, -1)
    return pl.pallas_call(
        functools.partial(_out_kernel, final=final),
        grid=(1,),
        in_specs=[full(x2d), full(mixed), full(wo_bf16), full(fin_g)],
        out_specs=full(x2d),
        out_shape=jax.ShapeDtypeStruct((m, d), F32),
        compiler_params=_cparams(("arbitrary",)),
        name="out_proj",
    )(x2d, mixed, wo_bf16, fin_g)


def _block_diag(pool_w):
    g_n, g, _ = pool_w.shape
    eye = jnp.eye(g_n, dtype=pool_w.dtype)
    return (eye[:, None, :, None] * pool_w[:, :, None, :]).reshape(g_n * g, g_n * g)


def _pick_tile(n, pref):
    t = min(n, pref)
    while n % t:
        t //= 2
    return t


def kernel(x_prompt, x_sample, mem_prompt, state_pool, state_shift, state_wkv, cache_mem_k, cache_mem_v, norm_g, w_in, w_out, pool_w, pool_scale, shift_mu, w0, w_w2, a0, w_a2, k_k, k_a, r_k, ln_x_g, ln_x_b, mem_norm_g, w_kv, final_norm_g):
    depth = w_in.shape[0]
    b, t, d = x_prompt.shape
    db = x_sample.shape[0]
    n_mem = mem_prompt.shape[1]
    pool_width = pool_scale.shape[1]
    rwkv_width = w0.shape[1]
    heads = rwkv_width // HEAD_DIM
    shift_width = shift_mu.shape[1]
    xa_width = w_kv.shape[2] // 2
    half = HEAD_DIM // 2
    assert 2 * b * heads == LANES, "the recurrence kernel places the (batch, head) chains on the lanes twice"
    splits = (pool_width, pool_width, shift_width, rwkv_width, xa_width, xa_width)
    start_pos = 16384

    w_in_b = w_in.astype(BF16)
    w_out_b = w_out.astype(BF16)
    w_kv_b = w_kv.astype(BF16)

    tm = _pick_tile(b * t, 512)
    tt_prep = _pick_tile(t, 256)
    tt_scan = _pick_tile(t, 64)
    tt_mix = _pick_tile(t, 256)
    bt = _pick_tile(db, 8)

    mem2d = mem_prompt.reshape(b * n_mem, d)
    memk, memv = [], []
    for l in range(depth):
        mk, mv = _project(mem2d, mem_norm_g[l], w_kv_b[l], (xa_width, xa_width), _pick_tile(b * n_mem, 512))
        memk.append(mk.reshape(b, n_mem, xa_width))
        memv.append(mv.reshape(b, n_mem, xa_width))

    x = x_prompt
    pool_p, shift_p, wkv_p = [], [], []
    for l in range(depth):
        outs = _project(x.reshape(b * t, d), norm_g[l], w_in_b[l], splits, tm)
        pv, pg, rw, rg, q, xg = [o.reshape(b, t, -1) for o in outs]
        pool_p.append(pv[:, t - POOL_BUF:])
        shift_p.append(rw[:, t - 1])
        r_, k_, v_, d_, kk_, b_ = _rwkv_prep(rw, shift_mu[l], w0[l], w_w2[l], a0[l], w_a2[l],
                                             k_k[l], k_a[l], tt_prep)
        rk_c = jnp.tile(r_k[l].T[:, None, None, :], (1, 2, b, 1)).reshape(HEAD_DIM, LANES)
        chain_i = lambda p: jnp.broadcast_to(
            p.reshape(heads, 2, half).transpose(2, 1, 0)[:, :, None, :], (half, 2, b, heads)).reshape(half, LANES)
        y_c, s_c = _rwkv_scan(_to_chain_j(d_, heads), _to_chain_j(b_, heads), _to_chain_j(k_, heads),
                              _to_chain_j(r_, heads), _to_chain_j(kk_, heads), _to_chain_i(v_, heads),
                              rk_c, chain_i(ln_x_g[l]), chain_i(ln_x_b[l]), tt_scan)
        wkv_p.append(s_c.reshape(HEAD_DIM, half, 2, b, heads).transpose(3, 4, 2, 1, 0)
                     .reshape(b, heads, HEAD_DIM, HEAD_DIM))
        yr = _from_chain_i(y_c, b, heads)
        x = _prompt_mix(x, pv, pg, yr, rg, q, xg, memk[l], memv[l], _block_diag(pool_w[l]), pool_scale[l],
                        w_out_b[l], final_norm_g, tt_mix, l == depth - 1)
    y_prompt = x

    xs = x_sample.reshape(db, d)
    pool_s, shift_s, wkv_s = [], [], []
    for l in range(depth):
        pv, pg, rw, rg, q, xg = _project(xs, norm_g[l], w_in_b[l], splits, _pick_tile(db, 128))
        pool_s.append(jnp.concatenate([state_pool[l][:, 1:], pv[:, None, :]], axis=1))
        shift_s.append(rw)
        vecs = (pool_scale[l], shift_mu[l], w0[l], a0[l], k_k[l], k_a[l], r_k[l], ln_x_g[l], ln_x_b[l])
        mixed, s_new = _sample_mix(
            pv, pg, rw, rg, q, xg, state_pool[l].transpose(1, 0, 2), state_shift[l], state_wkv[l],
            cache_mem_k[l].reshape(db, n_mem, xa_width), cache_mem_v[l].reshape(db, n_mem, xa_width),
            _block_diag(pool_w[l]), vecs, w_w2[l], w_a2[l], bt, start_pos)
        wkv_s.append(s_new)
        xs = _out_project(xs, mixed, w_out_b[l], final_norm_g, l == depth - 1)
    y_sample = xs.reshape(db, 1, d)

    xa_hd = xa_width // XA_HEADS
    kv_shape = (depth, b, n_mem, XA_HEADS, xa_hd)
    return (y_prompt, y_sample, jnp.stack(pool_p), jnp.stack(shift_p), jnp.stack(wkv_p),
            jnp.stack(memk).reshape(kv_shape), jnp.stack(memv).reshape(kv_shape),
            jnp.stack(pool_s), jnp.stack(shift_s), jnp.stack(wkv_s))
```

```python
import functools

import numpy as np
import jax
import jax.numpy as jnp
from jax import lax
from jax.experimental import pallas as pl
from jax.experimental.pallas import tpu as pltpu

F32 = jnp.float32
BF16 = jnp.bfloat16
HIGHEST = lax.Precision.HIGHEST

HEAD_DIM = 64
POOL_WINDOWS = (2, 4, 8, 16)
POOL_BUF = max(POOL_WINDOWS) - 1
POOL_HALO = 16
XA_HEADS = 4
DECAY_LORA = 64
ICLR_LORA = 64
EPS = 1e-6
GN_EPS = HEAD_DIM * 1e-5
LANES = 128
SUBLANES = 8
ROW_PITCH = 40
VMEM_LIMIT = 48 * 1024 * 1024
SCAN_VMEM_LIMIT = 56 * 1024 * 1024


def _cparams(sem, limit=VMEM_LIMIT):
    return pltpu.CompilerParams(dimension_semantics=sem, vmem_limit_bytes=limit)


def _silu(x):
    return x / (1.0 + jnp.exp(-x))


def _sigmoid(x):
    return 1.0 / (1.0 + jnp.exp(-x))


def _softplus(x):
    return jnp.maximum(x, 0.0) + jnp.log1p(jnp.exp(-jnp.abs(x)))


def _rms(x, g):
    return x * lax.rsqrt(jnp.mean(x * x, axis=-1, keepdims=True) + EPS) * g


def _proj_kernel(x_ref, g_ref, w_ref, *out_refs, splits):
    xn = _rms(x_ref[...], g_ref[...]).astype(BF16)
    off = 0
    for o_ref, n in zip(out_refs, splits):
        o_ref[...] = jnp.dot(xn, w_ref[:, off:off + n], preferred_element_type=F32)
        off += n


def _project(x2d, g, w_bf16, splits, tm):
    m, d = x2d.shape
    n = w_bf16.shape[1]
    return pl.pallas_call(
        functools.partial(_proj_kernel, splits=splits),
        grid=(m // tm,),
        in_specs=[pl.BlockSpec((tm, d), lambda i: (i, 0)),
                  pl.BlockSpec((1, d), lambda i: (0, 0)),
                  pl.BlockSpec((d, n), lambda i: (0, 0))],
        out_specs=[pl.BlockSpec((tm, s), lambda i: (i, 0)) for s in splits],
        out_shape=[jax.ShapeDtypeStruct((m, s), F32) for s in splits],
        compiler_params=_cparams(("arbitrary",)),
        name="proj",
    )(x2d, g.reshape(1, d), w_bf16)


def _pool_select(lane_group, vals):
    out = vals[-1]
    for gi in range(len(vals) - 2, -1, -1):
        out = jnp.where(lane_group == gi, vals[gi], out)
    return out


def _cross_attend_rows(qm, mk, mv):
    hd = mk.shape[1] // XA_HEADS
    s = lax.dot_general(qm, mk, (((1,), (1,)), ((), ())), preferred_element_type=F32) * (hd ** -0.5)
    p = jnp.exp(s - jnp.max(s, axis=-1, keepdims=True))
    p = p / jnp.sum(p, axis=-1, keepdims=True)
    return jnp.dot(p.astype(mv.dtype), mv, preferred_element_type=F32)


def _front_kernel(x_ref, g_ref, wn_ref, wt_ref, mu_ref, w0_ref, ww2_ref, a0_ref, wa2_ref, kk_ref, ka_ref,
                  mk_ref, mv_ref, wbd_ref, ps_ref,
                  d_o, b_o, k_o, r_o, kkn_o, v_o, gate_o, px_o, ptail_o, shift_o,
                  ext, carry, *, tt, width, heads):
    ti = pl.program_id(1)

    @pl.when(ti == 0)
    def _():
        ext[0:POOL_HALO, :] = jnp.zeros((POOL_HALO, ext.shape[1]), F32)
        carry[...] = jnp.zeros_like(carry)

    xn = _rms(x_ref[...], g_ref[...]).astype(BF16)
    nat = jnp.dot(xn, wn_ref[...], preferred_element_type=F32)
    cm = lax.dot_general(wt_ref[...], xn, (((1,), (1,)), ((), ())), preferred_element_type=F32)
    sw = mu_ref.shape[0]
    rwt = cm[0:sw]
    gate_o[...] = _silu(cm[sw:])

    prev = pltpu.roll(jnp.concatenate([carry[...], rwt], axis=1), 1, 1)[:, LANES:]
    carry[...] = rwt[:, tt - LANES:]
    shift_o[...] = rwt[:, tt - LANES:]
    rws = rwt + (prev - rwt) * mu_ref[...]
    r = rws[0:width]
    k = rws[width:2 * width]
    v = rws[2 * width:3 * width]
    wd = rws[3 * width:3 * width + DECAY_LORA]
    ad = rws[3 * width + DECAY_LORA:]
    wlin = w0_ref[...] + jnp.dot(ww2_ref[...], jnp.tanh(wd), precision=HIGHEST, preferred_element_type=F32)
    decay = jnp.exp(-jnp.exp(-_softplus(-wlin) - 0.5))
    a = _sigmoid(a0_ref[...] + jnp.dot(wa2_ref[...], ad, precision=HIGHEST, preferred_element_type=F32))
    kk3 = (k * kk_ref[...]).reshape(HEAD_DIM, heads, tt)
    nrm = jnp.maximum(jnp.sqrt(jnp.sum(kk3 * kk3, axis=0, keepdims=True)), 1e-12)
    kkn = (kk3 / nrm).reshape(width, tt)
    d_o[...] = decay
    b_o[...] = kkn * a
    k_o[...] = k * (1.0 + (a - 1.0) * ka_ref[...])
    r_o[...] = r
    kkn_o[...] = kkn
    v_o[...] = v

    pw = ps_ref.shape[1]
    xw = mk_ref.shape[1]
    pv = nat[:, 0:pw]
    pg = nat[:, pw:2 * pw]
    q = nat[:, 2 * pw:2 * pw + xw]
    xg = nat[:, 2 * pw + xw:]

    ext[POOL_HALO:POOL_HALO + tt, :] = pv
    sums = []
    acc = pv
    for back in range(1, max(POOL_WINDOWS)):
        acc = acc + ext[POOL_HALO - back:POOL_HALO - back + tt, :]
        if back + 1 in POOL_WINDOWS:
            sums.append(acc)
    tail = ext[tt:tt + POOL_HALO, :]
    ext[0:POOL_HALO, :] = tail
    ptail_o[...] = tail
    group = lax.broadcasted_iota(jnp.int32, (tt, pw), 1) // (pw // len(POOL_WINDOWS))
    pos = ti * tt + lax.broadcasted_iota(jnp.int32, (tt, pw), 0)
    win = _pool_select(group, [jnp.full((tt, pw), w, jnp.int32) for w in POOL_WINDOWS])
    cnt = jnp.minimum(pos + 1, win).astype(F32)
    pooled = _pool_select(group, sums) / cnt - pv
    pool_y = jnp.dot(pooled, wbd_ref[...], precision=HIGHEST, preferred_element_type=F32)
    px_o[:, 0:pw] = (pool_y * ps_ref[...] * _silu(pg)).astype(BF16)

    mk = mk_ref[...].astype(BF16)
    mv = mv_ref[...].astype(BF16)
    head = lax.broadcasted_iota(jnp.int32, (tt, xw), 1) // (xw // XA_HEADS)
    xa = jnp.zeros((tt, xw), F32)
    for h in range(XA_HEADS):
        qm = jnp.where(head == h, q, 0.0).astype(BF16)
        xa = jnp.where(head == h, _cross_attend_rows(qm, mk, mv), xa)
    px_o[:, pw:] = (xa * _silu(xg)).astype(BF16)


def _prompt_front(x, g, w_nat, w_t, cols, ww2_t, wa2_t, mk, mv, wbd, pscale, tt, width, heads):
    b, t, d = x.shape
    mu_c, w0_c, a0_c, kk_c, ka_c = cols
    sw = mu_c.shape[0]
    pw = pscale.shape[-1]
    xw = mk.shape[2]
    g = g.reshape(1, d)
    pscale = pscale.reshape(1, pw)
    const = lambda a: pl.BlockSpec(a.shape, lambda bi, ti: (0,) * a.ndim)
    cmaj = pl.BlockSpec((None, width, tt), lambda bi, ti: (bi, 0, ti))
    perb = lambda s: pl.BlockSpec((None,) + s, lambda bi, ti: (bi, 0, 0))
    return pl.pallas_call(
        functools.partial(_front_kernel, tt=tt, width=width, heads=heads),
        grid=(b, t // tt),
        in_specs=[pl.BlockSpec((None, tt, d), lambda bi, ti: (bi, ti, 0)), const(g), const(w_nat), const(w_t),
                  const(mu_c), const(w0_c), const(ww2_t), const(a0_c), const(wa2_t), const(kk_c), const(ka_c),
                  perb(mk.shape[1:]), perb(mv.shape[1:]), const(wbd), const(pscale)],
        out_specs=[cmaj] * 7 + [pl.BlockSpec((None, tt, pw + xw), lambda bi, ti: (bi, ti, 0)),
                                perb((POOL_HALO, pw)), perb((sw, LANES))],
        out_shape=[jax.ShapeDtypeStruct((b, width, t), F32)] * 7
        + [jax.ShapeDtypeStruct((b, t, pw + xw), BF16),
           jax.ShapeDtypeStruct((b, POOL_HALO, pw), F32),
           jax.ShapeDtypeStruct((b, sw, LANES), F32)],
        scratch_shapes=[pltpu.VMEM((tt + POOL_HALO, pw), F32), pltpu.VMEM((sw, LANES), F32)],
        compiler_params=_cparams(("arbitrary", "arbitrary")),
        name="front",
    )(x, g, w_nat, w_t, mu_c, w0_c, ww2_t, a0_c, wa2_t, kk_c, ka_c, mk, mv, wbd, pscale)


def _scan_kernel(d_ref, b_ref, k_ref, r_ref, kk_ref, v_ref, rk_ref, g_ref, be_ref,
                 y_ref, s_ref, wj, bj, kj, rj, kkj, vs, ys, u_scr, *, n, heads, nb):
    half = n // 2
    chains = nb * heads

    @pl.when(pl.program_id(0) == 0)
    def _():
        s_ref[...] = jnp.zeros_like(s_ref)

    def relayout_j(j, carry):
        row0 = pl.multiple_of(j * heads, SUBLANES)
        for src, dst in ((d_ref, wj), (b_ref, bj), (k_ref, kj), (r_ref, rj), (kk_ref, kkj)):
            m = src[:, pl.ds(row0, heads), :].reshape(chains, LANES)
            dst[j] = jnp.concatenate([m, m], axis=0).T
        return carry

    lax.fori_loop(0, n, relayout_j, 0)
    for il in range(half):
        lo = v_ref[:, (2 * il) * heads:(2 * il + 1) * heads, :].reshape(chains, LANES)
        hi = v_ref[:, (2 * il + 1) * heads:(2 * il + 2) * heads, :].reshape(chains, LANES)
        vs[pl.ds(il, LANES, stride=ROW_PITCH), :] = jnp.concatenate([lo, hi], axis=0).T

    acc = jnp.zeros((half, LANES), F32)
    for j in range(n):
        acc = acc + s_ref[j] * kkj[j, 0:1, :]
    u_scr[...] = acc

    def step(t, carry):
        tn = jnp.minimum(t + 1, LANES - 1)
        row = pl.multiple_of(t * ROW_PITCH, SUBLANES)
        u = u_scr[...]
        v = vs[pl.ds(row, half), :]
        y = jnp.zeros((half, LANES), F32)
        nacc = jnp.zeros((half, LANES), F32)
        for j in range(n):
            s = (s_ref[j] * wj[j, pl.ds(t, 1), :] - u * bj[j, pl.ds(t, 1), :]
                 + v * kj[j, pl.ds(t, 1), :])
            s_ref[j] = s
            y = y + s * rj[j, pl.ds(t, 1), :]
            nacc = nacc + s * kkj[j, pl.ds(tn, 1), :]
        u_scr[...] = nacc
        ys[pl.ds(row, half), :] = y
        return carry

    lax.fori_loop(0, LANES, step, 0)

    for il in range(half):
        yt = ys[pl.ds(il, LANES, stride=ROW_PITCH), :].T
        for ih in range(2):
            for bi in range(nb):
                src0 = (ih * nb + bi) * heads
                dst0 = (2 * il + ih) * heads
                y_ref[bi, dst0:dst0 + heads, :] = yt[src0:src0 + heads, :]
    for bi in range(nb):
        y3 = y_ref[bi].reshape(n, heads, LANES)
        d = y3 - jnp.sum(y3, axis=0, keepdims=True) * (1.0 / n)
        var = jnp.sum(d * d, axis=0, keepdims=True) * (1.0 / n)
        bonus = jnp.sum((r_ref[bi] * k_ref[bi] * rk_ref[...]).reshape(n, heads, LANES), axis=0, keepdims=True)
        out = d * lax.rsqrt(var + GN_EPS) * g_ref[...] + be_ref[...] + bonus * v_ref[bi].reshape(n, heads, LANES)
        y_ref[bi] = out.reshape(n * heads, LANES)


def _rwkv_scan(d_t, b_t, k_t, r_t, kk_t, v_t, rk_c, g_c, be_c, heads):
    nb, width, t = d_t.shape
    n = width // heads
    half = n // 2
    sync = pl.BlockSpec((nb, width, LANES), lambda ti: (0, 0, ti), pipeline_mode=pl.Buffered(1))
    const = lambda a: pl.BlockSpec(a.shape, lambda ti: (0,) * a.ndim)
    jscr = pltpu.VMEM((n, LANES, LANES), F32)
    iscr = pltpu.VMEM((LANES * ROW_PITCH, LANES), F32)
    return pl.pallas_call(
        functools.partial(_scan_kernel, n=n, heads=heads, nb=nb),
        grid=(t // LANES,),
        in_specs=[sync] * 6 + [const(rk_c), const(g_c), const(be_c)],
        out_specs=[pl.BlockSpec((nb, width, LANES), lambda ti: (0, 0, ti)),
                   pl.BlockSpec((n, half, LANES), lambda ti: (0, 0, 0))],
        out_shape=[jax.ShapeDtypeStruct((nb, width, t), F32),
                   jax.ShapeDtypeStruct((n, half, LANES), F32)],
        scratch_shapes=[jscr] * 5 + [iscr, iscr, pltpu.VMEM((half, LANES), F32)],
        compiler_params=_cparams(("arbitrary",), SCAN_VMEM_LIMIT),
        name="rwkv_scan",
    )(d_t, b_t, k_t, r_t, kk_t, v_t, rk_c, g_c, be_c)


def _back_kernel(x_ref, px_ref, yt_ref, gate_ref, wpx_ref, wrw_ref, fg_ref, o_ref, *, final):
    yg = (yt_ref[...] * gate_ref[...]).T.astype(BF16)
    out = x_ref[...] + jnp.dot(px_ref[...], wpx_ref[...], preferred_element_type=F32)
    out = out + jnp.dot(yg, wrw_ref[...], preferred_element_type=F32)
    if final:
        out = _rms(out, fg_ref[...])
    o_ref[...] = out


def _prompt_back(x, px, y_t, gate_t, w_px, w_rw, fin_g, tt, final):
    b, t, d = x.shape
    width = y_t.shape[1]
    fin_g = fin_g.reshape(1, d)
    const = lambda a: pl.BlockSpec(a.shape, lambda bi, ti: (0,) * a.ndim)
    tile = lambda c: pl.BlockSpec((None, tt, c), lambda bi, ti: (bi, ti, 0))
    cmaj = pl.BlockSpec((None, width, tt), lambda bi, ti: (bi, 0, ti))
    return pl.pallas_call(
        functools.partial(_back_kernel, final=final),
        grid=(b, t // tt),
        in_specs=[tile(d), tile(px.shape[2]), cmaj, cmaj, const(w_px), const(w_rw), const(fin_g)],
        out_specs=tile(d),
        out_shape=jax.ShapeDtypeStruct((b, t, d), F32),
        compiler_params=_cparams(("arbitrary", "arbitrary")),
        name="back",
    )(x, px, y_t, gate_t, w_px, w_rw, fin_g)


def _rwkv_vectors(rws, w0, w_w2, a0, w_a2, k_k, k_a, width):
    r = rws[:, :width]
    k = rws[:, width:2 * width]
    v = rws[:, 2 * width:3 * width]
    wd = rws[:, 3 * width:3 * width + DECAY_LORA]
    ad = rws[:, 3 * width + DECAY_LORA:]
    wlin = w0 + jnp.dot(jnp.tanh(wd), w_w2, precision=HIGHEST, preferred_element_type=F32)
    decay = jnp.exp(-jnp.exp(-_softplus(-wlin) - 0.5))
    a = _sigmoid(a0 + jnp.dot(ad, w_a2, precision=HIGHEST, preferred_element_type=F32))
    kk_raw = k * k_k
    k = k * (1.0 + (a - 1.0) * k_a)
    return r, k, v, decay, kk_raw, a


def _sample_kernel(pv_ref, pg_ref, rw_ref, rg_ref, q_ref, xg_ref, pbuf_ref, sprev_ref, s_ref,
                   ck_ref, cv_ref, wbd_ref, ps_ref, mu_ref, w0_ref, ww2_ref, a0_ref, wa2_ref,
                   kk_ref, ka_ref, rk_ref, g_ref, be_ref,
                   mixed_ref, so_ref, y_scr, xa_scr, *, bt, start_pos):
    n = HEAD_DIM
    pv = pv_ref[...]
    pw = pv.shape[1]
    sums = []
    acc = pv
    for back in range(1, max(POOL_WINDOWS)):
        acc = acc + pbuf_ref[POOL_BUF - back]
        if back + 1 in POOL_WINDOWS:
            sums.append(acc)
    group = lax.broadcasted_iota(jnp.int32, (bt, pw), 1) // (pw // len(POOL_WINDOWS))
    cnts = [jnp.full((bt, pw), float(min(start_pos + 1, w)), F32) for w in POOL_WINDOWS]
    pooled = _pool_select(group, sums) / _pool_select(group, cnts) - pv
    pool_y = jnp.dot(pooled, wbd_ref[...], precision=HIGHEST, preferred_element_type=F32)
    pool_y = pool_y * ps_ref[...] * _silu(pg_ref[...])

    rw = rw_ref[...]
    rws = rw + (sprev_ref[...] - rw) * mu_ref[...]
    width = w0_ref.shape[1]
    heads = width // n
    r, k, v, decay, kk_raw, a = _rwkv_vectors(
        rws, w0_ref[...], ww2_ref[...], a0_ref[...], wa2_ref[...], kk_ref[...], ka_ref[...], width)
    eye = (lax.broadcasted_iota(jnp.int32, (n, n), 0) == lax.broadcasted_iota(jnp.int32, (n, n), 1))[None]
    for h in range(heads):
        sl = slice(h * n, (h + 1) * n)
        r_h, k_h, v_h = r[:, sl], k[:, sl], v[:, sl]
        kk_h = kk_raw[:, sl]
        kk_h = kk_h / jnp.maximum(jnp.sqrt(jnp.sum(kk_h * kk_h, axis=-1, keepdims=True)), 1e-12)
        b_h = kk_h * a[:, sl]
        s = s_ref[:, h]
        s_kk = jnp.sum(s * kk_h[:, None, :], axis=-1, keepdims=True)
        v_col = jnp.sum(jnp.where(eye, v_h[:, None, :], 0.0), axis=-1, keepdims=True)
        s = s * decay[:, sl][:, None, :] - s_kk * b_h[:, None, :] + v_col * k_h[:, None, :]
        so_ref[:, h] = s
        y_col = jnp.sum(s * r_h[:, None, :], axis=-1, keepdims=True)
        y = jnp.sum(jnp.where(eye, y_col, 0.0), axis=1)
        d = y - jnp.mean(y, axis=-1, keepdims=True)
        var = jnp.mean(d * d, axis=-1, keepdims=True)
        y = d * lax.rsqrt(var + GN_EPS) * g_ref[:, sl] + be_ref[:, sl]
        bonus = jnp.sum(r_h * k_h * rk_ref[:, sl], axis=-1, keepdims=True)
        y_scr[:, sl] = y + bonus * v_h
    rwkv_y = y_scr[...] * _silu(rg_ref[...])

    q = q_ref[...]
    xw = q.shape[1]
    row_h = lax.broadcasted_iota(jnp.int32, (SUBLANES, xw), 0)
    lane_h = lax.broadcasted_iota(jnp.int32, (SUBLANES, xw), 1) // (xw // XA_HEADS)
    for bi in range(bt):
        qm = jnp.where(row_h == lane_h, q[bi:bi + 1, :], 0.0)
        o8 = _cross_attend_rows(qm, ck_ref[bi], cv_ref[bi])
        xa_scr[bi:bi + 1, :] = jnp.sum(jnp.where(row_h == lane_h, o8, 0.0), axis=0, keepdims=True)
    xa_y = xa_scr[...] * _silu(xg_ref[...])

    mixed_ref[:, 0:pw] = pool_y
    mixed_ref[:, pw:pw + width] = rwkv_y
    mixed_ref[:, pw + width:] = xa_y


def _sample_mix(pv, pg, rw, rg, q, xg, pbuf_t, sprev, s0, ck, cv, wbd, vecs, ww2, wa2, bt, start_pos):
    b = pv.shape[0]
    row = lambda c: pl.BlockSpec((bt, c), lambda i: (i, 0))
    const = lambda a: pl.BlockSpec(a.shape, lambda i: (0,) * a.ndim)
    lead = lambda a: pl.BlockSpec((bt,) + a.shape[1:], lambda i: (i,) + (0,) * (a.ndim - 1))
    pscale, mu, w0, a0, k_k, k_a, r_k, ln_g, ln_b = [a.reshape(1, -1) for a in vecs]
    mixw = pv.shape[1] + w0.shape[1] + q.shape[1]
    return pl.pallas_call(
        functools.partial(_sample_kernel, bt=bt, start_pos=start_pos),
        grid=(b // bt,),
        in_specs=[row(pv.shape[1]), row(pg.shape[1]), row(rw.shape[1]), row(rg.shape[1]),
                  row(q.shape[1]), row(xg.shape[1]),
                  pl.BlockSpec((POOL_BUF, bt, pv.shape[1]), lambda i: (0, i, 0)),
                  row(sprev.shape[1]), lead(s0), lead(ck), lead(cv),
                  const(wbd), const(pscale), const(mu), const(w0), const(ww2), const(a0), const(wa2),
                  const(k_k), const(k_a), const(r_k), const(ln_g), const(ln_b)],
        out_specs=[row(mixw), lead(s0)],
        out_shape=[jax.ShapeDtypeStruct((b, mixw), F32), jax.ShapeDtypeStruct(s0.shape, F32)],
        scratch_shapes=[pltpu.VMEM((bt, w0.shape[1]), F32), pltpu.VMEM((bt, q.shape[1]), F32)],
        compiler_params=_cparams(("arbitrary",)),
        name="sample_mix",
    )(pv, pg, rw, rg, q, xg, pbuf_t, sprev, s0, ck, cv, wbd, pscale, mu, w0, ww2, a0, wa2,
      k_k, k_a, r_k, ln_g, ln_b)


def _out_kernel(x_ref, m_ref, wo_ref, fg_ref, o_ref, *, final):
    out = x_ref[...] + jnp.dot(m_ref[...].astype(BF16), wo_ref[...], preferred_element_type=F32)
    if final:
        out = _rms(out, fg_ref[...])
    o_ref[...] = out


def _out_project(x2d, mixed, wo_bf16, fin_g, final):
    m, d = x2d.shape
    full = lambda a: pl.BlockSpec(a.shape, lambda i: (0,) * a.ndim)
    fin_g = fin_g.reshape(1, -1)
    return pl.pallas_call(
        functools.partial(_out_kernel, final=final),
        grid=(1,),
        in_specs=[full(x2d), full(mixed), full(wo_bf16), full(fin_g)],
        out_specs=full(x2d),
        out_shape=jax.ShapeDtypeStruct((m, d), F32),
        compiler_params=_cparams(("arbitrary",)),
        name="out_proj",
    )(x2d, mixed, wo_bf16, fin_g)


def _block_diag(pool_w):
    g_n, g, _ = pool_w.shape
    eye = jnp.eye(g_n, dtype=pool_w.dtype)
    return (eye[:, None, :, None] * pool_w[:, :, None, :]).reshape(g_n * g, g_n * g)


def _pick_tile(n, pref):
    t = min(n, pref)
    while n % t:
        t //= 2
    return t


def _channel_orders(heads):
    n = HEAD_DIM
    j = np.arange(n)[:, None]
    h = np.arange(heads)[None, :]
    perm_j = (h * n + j).reshape(-1)
    il = np.arange(n // 2)[:, None, None]
    ih = np.arange(2)[None, :, None]
    h3 = np.arange(heads)[None, None, :]
    perm_i = (h3 * n + ih * (n // 2) + il).reshape(-1)
    return perm_j, perm_i


def kernel(x_prompt, x_sample, mem_prompt, state_pool, state_shift, state_wkv, cache_mem_k, cache_mem_v, norm_g, w_in, w_out, pool_w, pool_scale, shift_mu, w0, w_w2, a0, w_a2, k_k, k_a, r_k, ln_x_g, ln_x_b, mem_norm_g, w_kv, final_norm_g):
    depth = w_in.shape[0]
    b, t, d = x_prompt.shape
    db = x_sample.shape[0]
    n_mem = mem_prompt.shape[1]
    pw = pool_scale.shape[1]
    width = w0.shape[1]
    heads = width // HEAD_DIM
    sw = shift_mu.shape[1]
    xw = w_kv.shape[2] // 2
    half = HEAD_DIM // 2
    assert 2 * b * heads == LANES, "the recurrence kernel places the (batch, head) chains on the lanes twice"
    assert t % LANES == 0
    splits = (pw, pw, sw, width, xw, xw)
    start_pos = 16384

    w_in_b = w_in.astype(BF16)
    w_out_b = w_out.astype(BF16)
    w_kv_b = w_kv.astype(BF16)

    perm_j, perm_i = _channel_orders(heads)
    rw_perm = np.concatenate([perm_j, width + perm_j, 2 * width + perm_i, 3 * width + np.arange(sw - 3 * width)])
    rw_inv = np.argsort(rw_perm)
    rw0 = 2 * pw
    rg0 = rw0 + sw
    t_cols = np.concatenate([rw0 + rw_perm, rg0 + perm_i])
    n_cols = np.concatenate([np.arange(0, rw0), np.arange(rg0 + width, rg0 + width + 2 * xw)])
    col = lambda a: a.reshape(-1, 1)

    tt = _pick_tile(t, 256)
    bt = _pick_tile(db, 8)

    mem2d = mem_prompt.reshape(b * n_mem, d)
    memk, memv = [], []
    for l in range(depth):
        mk, mv = _project(mem2d, mem_norm_g[l], w_kv_b[l], (xw, xw), _pick_tile(b * n_mem, 512))
        memk.append(mk.reshape(b, n_mem, xw))
        memv.append(mv.reshape(b, n_mem, xw))

    x = x_prompt
    pool_p, shift_p, wkv_p = [], [], []
    for l in range(depth):
        w_t = w_in_b[l][:, t_cols].T
        w_nat = w_in_b[l][:, n_cols]
        cols = (col(shift_mu[l][rw_perm]), col(w0[l][perm_j]), col(a0[l][perm_j]),
                col(k_k[l][perm_j]), col(k_a[l][perm_j]))
        outs = _prompt_front(x, norm_g[l], w_nat, w_t, cols, w_w2[l][:, perm_j].T, w_a2[l][:, perm_j].T,
                             memk[l], memv[l], _block_diag(pool_w[l]), pool_scale[l], tt, width, heads)
        d_t, b_t, k_t, r_t, kk_t, v_t, gate_t, px, ptail, shift_t = outs
        pool_p.append(ptail[:, POOL_HALO - POOL_BUF:])
        shift_p.append(shift_t[:, :, LANES - 1][:, rw_inv])
        gn = lambda p: p[perm_i].reshape(HEAD_DIM, heads, 1)
        y_t, s_c = _rwkv_scan(d_t, b_t, k_t, r_t, kk_t, v_t, col(r_k[l].reshape(-1)[perm_j]),
                              gn(ln_x_g[l]), gn(ln_x_b[l]), heads)
        wkv_p.append(s_c.reshape(HEAD_DIM, half, 2, b, heads).transpose(3, 4, 2, 1, 0)
                     .reshape(b, heads, HEAD_DIM, HEAD_DIM))
        w_px = jnp.concatenate([w_out_b[l][0:pw], w_out_b[l][pw + width:]], axis=0)
        w_rw = w_out_b[l][pw:pw + width][perm_i]
        x = _prompt_back(x, px, y_t, gate_t, w_px, w_rw, final_norm_g, tt, l == depth - 1)
    y_prompt = x

    xs = x_sample.reshape(db, d)
    pool_s, shift_s, wkv_s = [], [], []
    for l in range(depth):
        pv, pg, rw, rg, q, xg = _project(xs, norm_g[l], w_in_b[l], splits, _pick_tile(db, 128))
        pool_s.append(jnp.concatenate([state_pool[l][:, 1:], pv[:, None, :]], axis=1))
        shift_s.append(rw)
        vecs = (pool_scale[l], shift_mu[l], w0[l], a0[l], k_k[l], k_a[l], r_k[l], ln_x_g[l], ln_x_b[l])
        mixed, s_new = _sample_mix(
            pv, pg, rw, rg, q, xg, state_pool[l].transpose(1, 0, 2), state_shift[l], state_wkv[l],
            cache_mem_k[l].reshape(db, n_mem, xw), cache_mem_v[l].reshape(db, n_mem, xw),
            _block_diag(pool_w[l]), vecs, w_w2[l], w_a2[l], bt, start_pos)
        wkv_s.append(s_new)
        xs = _out_project(xs, mixed, w_out_b[l], final_norm_g, l == depth - 1)
    y_sample = xs.reshape(db, 1, d)

    kv_shape = (depth, b, n_mem, XA_HEADS, xw // XA_HEADS)
    return (y_prompt, y_sample, jnp.stack(pool_p), jnp.stack(shift_p), jnp.stack(wkv_p),
            jnp.stack(memk).reshape(kv_shape), jnp.stack(memv).reshape(kv_shape),
            jnp.stack(pool_s), jnp.stack(shift_s), jnp.stack(wkv_s))
```

```python
import functools

import numpy as np
import jax
import jax.numpy as jnp
from jax import lax
from jax.experimental import pallas as pl
from jax.experimental.pallas import tpu as pltpu

F32 = jnp.float32
BF16 = jnp.bfloat16
HIGHEST = lax.Precision.HIGHEST

HEAD_DIM = 64
POOL_WINDOWS = (2, 4, 8, 16)
POOL_BUF = max(POOL_WINDOWS) - 1
POOL_HALO = 16
XA_HEADS = 4
DECAY_LORA = 64
ICLR_LORA = 64
EPS = 1e-6
GN_EPS = HEAD_DIM * 1e-5
LANES = 128
SUBLANES = 8
ROW_PITCH = 40
VMEM_LIMIT = 48 * 1024 * 1024
SCAN_VMEM_LIMIT = 60 * 1024 * 1024


def _cparams(sem, limit=VMEM_LIMIT):
    return pltpu.CompilerParams(dimension_semantics=sem, vmem_limit_bytes=limit)


def _silu(x):
    return x / (1.0 + jnp.exp(-x))


def _sigmoid(x):
    return 1.0 / (1.0 + jnp.exp(-x))


def _softplus(x):
    return jnp.maximum(x, 0.0) + jnp.log1p(jnp.exp(-jnp.abs(x)))


def _split_bf16(x):
    hi = x.astype(BF16)
    return hi, (x - hi.astype(F32)).astype(BF16)


def _dot3(a, b):
    a_hi, a_lo = _split_bf16(a)
    b_hi, b_lo = _split_bf16(b)
    dot = functools.partial(jnp.dot, preferred_element_type=F32)
    return dot(a_hi, b_hi) + (dot(a_hi, b_lo) + dot(a_lo, b_hi))


def _rms(x, g):
    return x * lax.rsqrt(jnp.mean(x * x, axis=-1, keepdims=True) + EPS) * g


def _proj_kernel(x_ref, g_ref, w_ref, *out_refs, splits):
    xn = _rms(x_ref[...], g_ref[...]).astype(BF16)
    off = 0
    for o_ref, n in zip(out_refs, splits):
        o_ref[...] = jnp.dot(xn, w_ref[:, off:off + n], preferred_element_type=F32)
        off += n


def _project(x2d, g, w_bf16, splits, tm):
    m, d = x2d.shape
    n = w_bf16.shape[1]
    return pl.pallas_call(
        functools.partial(_proj_kernel, splits=splits),
        grid=(m // tm,),
        in_specs=[pl.BlockSpec((tm, d), lambda i: (i, 0)),
                  pl.BlockSpec((1, d), lambda i: (0, 0)),
                  pl.BlockSpec((d, n), lambda i: (0, 0))],
        out_specs=[pl.BlockSpec((tm, s), lambda i: (i, 0)) for s in splits],
        out_shape=[jax.ShapeDtypeStruct((m, s), F32) for s in splits],
        compiler_params=_cparams(("arbitrary",)),
        name="proj",
    )(x2d, g.reshape(1, d), w_bf16)


def _pool_select(lane_group, vals):
    out = vals[-1]
    for gi in range(len(vals) - 2, -1, -1):
        out = jnp.where(lane_group == gi, vals[gi], out)
    return out


def _cross_attend_rows(qm, mk, mv):
    hd = mk.shape[1] // XA_HEADS
    s = lax.dot_general(qm, mk, (((1,), (1,)), ((), ())), preferred_element_type=F32) * (hd ** -0.5)
    p = jnp.exp(s - jnp.max(s, axis=-1, keepdims=True))
    p = p / jnp.sum(p, axis=-1, keepdims=True)
    return jnp.dot(p.astype(mv.dtype), mv, preferred_element_type=F32)


def _front_kernel(x_ref, g_ref, wn_ref, wt_ref, mu_ref, w0_ref, ww2_ref, a0_ref, wa2_ref, kk_ref, ka_ref,
                  mk_ref, mv_ref, wbd_ref, ps_ref,
                  d_o, b_o, k_o, r_o, kkn_o, v_o, gate_o, px_o, ptail_o, shift_o,
                  ext, carry, *, tt, width, heads):
    ti = pl.program_id(1)

    @pl.when(ti == 0)
    def _():
        ext[0:POOL_HALO, :] = jnp.zeros((POOL_HALO, ext.shape[1]), F32)
        carry[...] = jnp.zeros_like(carry)

    xn = _rms(x_ref[...], g_ref[...]).astype(BF16)
    nat = jnp.dot(xn, wn_ref[...], preferred_element_type=F32)
    cm = lax.dot_general(wt_ref[...], xn, (((1,), (1,)), ((), ())), preferred_element_type=F32)
    sw = mu_ref.shape[0]
    rwt = cm[0:sw]
    gate_o[...] = _silu(cm[sw:])

    prev = pltpu.roll(jnp.concatenate([carry[...], rwt], axis=1), 1, 1)[:, LANES:]
    carry[...] = rwt[:, tt - LANES:]
    shift_o[...] = rwt[:, tt - LANES:]
    rws = rwt + (prev - rwt) * mu_ref[...]
    r = rws[0:width]
    k = rws[width:2 * width]
    v = rws[2 * width:3 * width]
    wd = rws[3 * width:3 * width + DECAY_LORA]
    ad = rws[3 * width + DECAY_LORA:]
    wlin = w0_ref[...] + _dot3(ww2_ref[...], jnp.tanh(wd))
    decay = jnp.exp(-jnp.exp(-_softplus(-wlin) - 0.5))
    a = _sigmoid(a0_ref[...] + _dot3(wa2_ref[...], ad))
    kk3 = (k * kk_ref[...]).reshape(HEAD_DIM, heads, tt)
    nrm = jnp.maximum(jnp.sqrt(jnp.sum(kk3 * kk3, axis=0, keepdims=True)), 1e-12)
    kkn = (kk3 / nrm).reshape(width, tt)
    d_o[...] = decay
    b_o[...] = kkn * a
    k_o[...] = k * (1.0 + (a - 1.0) * ka_ref[...])
    r_o[...] = r
    kkn_o[...] = kkn
    v_o[...] = v

    pw = ps_ref.shape[1]
    xw = mk_ref.shape[1]
    pv = nat[:, 0:pw]
    pg = nat[:, pw:2 * pw]
    q = nat[:, 2 * pw:2 * pw + xw]
    xg = nat[:, 2 * pw + xw:]

    ext[POOL_HALO:POOL_HALO + tt, :] = pv
    sums = []
    acc = pv
    for back in range(1, max(POOL_WINDOWS)):
        acc = acc + ext[POOL_HALO - back:POOL_HALO - back + tt, :]
        if back + 1 in POOL_WINDOWS:
            sums.append(acc)
    tail = ext[tt:tt + POOL_HALO, :]
    ext[0:POOL_HALO, :] = tail
    ptail_o[...] = tail
    group = lax.broadcasted_iota(jnp.int32, (tt, pw), 1) // (pw // len(POOL_WINDOWS))
    pos = ti * tt + lax.broadcasted_iota(jnp.int32, (tt, pw), 0)
    win = _pool_select(group, [jnp.full((tt, pw), w, jnp.int32) for w in POOL_WINDOWS])
    cnt = jnp.minimum(pos + 1, win).astype(F32)
    pooled = _pool_select(group, sums) / cnt - pv
    pool_y = _dot3(pooled, wbd_ref[...])
    px_o[:, 0:pw] = (pool_y * ps_ref[...] * _silu(pg)).astype(BF16)

    mk = mk_ref[...].astype(BF16)
    mv = mv_ref[...].astype(BF16)
    head = lax.broadcasted_iota(jnp.int32, (tt, xw), 1) // (xw // XA_HEADS)
    xa = jnp.zeros((tt, xw), F32)
    for h in range(XA_HEADS):
        qm = jnp.where(head == h, q, 0.0).astype(BF16)
        xa = jnp.where(head == h, _cross_attend_rows(qm, mk, mv), xa)
    px_o[:, pw:] = (xa * _silu(xg)).astype(BF16)


def _prompt_front(x, g, w_nat, w_t, cols, ww2_t, wa2_t, mk, mv, wbd, pscale, tt, width, heads):
    b, t, d = x.shape
    mu_c, w0_c, a0_c, kk_c, ka_c = cols
    sw = mu_c.shape[0]
    pw = pscale.shape[-1]
    xw = mk.shape[2]
    g = g.reshape(1, d)
    pscale = pscale.reshape(1, pw)
    const = lambda a: pl.BlockSpec(a.shape, lambda bi, ti: (0,) * a.ndim)
    cmaj = pl.BlockSpec((None, width, tt), lambda bi, ti: (bi, 0, ti))
    perb = lambda s: pl.BlockSpec((None,) + s, lambda bi, ti: (bi, 0, 0))
    return pl.pallas_call(
        functools.partial(_front_kernel, tt=tt, width=width, heads=heads),
        grid=(b, t // tt),
        in_specs=[pl.BlockSpec((None, tt, d), lambda bi, ti: (bi, ti, 0)), const(g), const(w_nat), const(w_t),
                  const(mu_c), const(w0_c), const(ww2_t), const(a0_c), const(wa2_t), const(kk_c), const(ka_c),
                  perb(mk.shape[1:]), perb(mv.shape[1:]), const(wbd), const(pscale)],
        out_specs=[cmaj] * 7 + [pl.BlockSpec((None, tt, pw + xw), lambda bi, ti: (bi, ti, 0)),
                                perb((POOL_HALO, pw)), perb((sw, LANES))],
        out_shape=[jax.ShapeDtypeStruct((b, width, t), F32)] * 7
        + [jax.ShapeDtypeStruct((b, t, pw + xw), BF16),
           jax.ShapeDtypeStruct((b, POOL_HALO, pw), F32),
           jax.ShapeDtypeStruct((b, sw, LANES), F32)],
        scratch_shapes=[pltpu.VMEM((tt + POOL_HALO, pw), F32), pltpu.VMEM((sw, LANES), F32)],
        compiler_params=_cparams(("arbitrary", "arbitrary")),
        name="front",
    )(x, g, w_nat, w_t, mu_c, w0_c, ww2_t, a0_c, wa2_t, kk_c, ka_c, mk, mv, wbd, pscale)


def _scan_kernel(d_ref, b_ref, k_ref, r_ref, kk_ref, v_ref, rk_ref, g_ref, be_ref,
                 y_ref, s_ref, wj, bj, kj, rj, kkj, vy, u_scr, *, n, heads, nb):
    half = n // 2
    chains = nb * heads

    @pl.when(pl.program_id(0) == 0)
    def _():
        s_ref[...] = jnp.zeros_like(s_ref)

    def relayout_j(j, carry):
        row0 = pl.multiple_of(j * heads, SUBLANES)
        for src, dst in ((d_ref, wj), (b_ref, bj), (k_ref, kj), (r_ref, rj), (kk_ref, kkj)):
            m = src[:, pl.ds(row0, heads), :].reshape(chains, LANES)
            dst[j] = jnp.concatenate([m, m], axis=0).T
        return carry

    lax.fori_loop(0, n, relayout_j, 0)
    for il in range(half):
        lo = v_ref[:, (2 * il) * heads:(2 * il + 1) * heads, :].reshape(chains, LANES)
        hi = v_ref[:, (2 * il + 1) * heads:(2 * il + 2) * heads, :].reshape(chains, LANES)
        vy[pl.ds(il, LANES, stride=ROW_PITCH), :] = jnp.concatenate([lo, hi], axis=0).T

    acc = jnp.zeros((half, LANES), F32)
    for j in range(n):
        acc = acc + s_ref[j] * kkj[j, 0:1, :]
    u_scr[...] = acc

    def step(t, carry):
        tn = jnp.minimum(t + 1, LANES - 1)
        row = pl.multiple_of(t * ROW_PITCH, SUBLANES)
        u = u_scr[...]
        v = vy[pl.ds(row, half), :]
        y = jnp.zeros((half, LANES), F32)
        nacc = jnp.zeros((half, LANES), F32)
        for j in range(n):
            s = (s_ref[j] * wj[j, pl.ds(t, 1), :] - u * bj[j, pl.ds(t, 1), :]
                 + v * kj[j, pl.ds(t, 1), :])
            s_ref[j] = s
            y = y + s * rj[j, pl.ds(t, 1), :]
            nacc = nacc + s * kkj[j, pl.ds(tn, 1), :]
        u_scr[...] = nacc
        vy[pl.ds(row, half), :] = y
        return carry

    lax.fori_loop(0, LANES, step, 0)

    for il in range(half):
        yt = vy[pl.ds(il, LANES, stride=ROW_PITCH), :].T
        for ih in range(2):
            for bi in range(nb):
                src0 = (ih * nb + bi) * heads
                dst0 = (2 * il + ih) * heads
                y_ref[bi, dst0:dst0 + heads, :] = yt[src0:src0 + heads, :]
    for bi in range(nb):
        y3 = y_ref[bi].reshape(n, heads, LANES)
        d = y3 - jnp.sum(y3, axis=0, keepdims=True) * (1.0 / n)
        var = jnp.sum(d * d, axis=0, keepdims=True) * (1.0 / n)
        bonus = jnp.sum((r_ref[bi] * k_ref[bi] * rk_ref[...]).reshape(n, heads, LANES), axis=0, keepdims=True)
        out = d * lax.rsqrt(var + GN_EPS) * g_ref[...] + be_ref[...] + bonus * v_ref[bi].reshape(n, heads, LANES)
        y_ref[bi] = out.reshape(n * heads, LANES)


def _rwkv_scan(d_t, b_t, k_t, r_t, kk_t, v_t, rk_c, g_c, be_c, heads):
    nb, width, t = d_t.shape
    n = width // heads
    half = n // 2
    tile = pl.BlockSpec((nb, width, LANES), lambda ti: (0, 0, ti))
    const = lambda a: pl.BlockSpec(a.shape, lambda ti: (0,) * a.ndim)
    jscr = pltpu.VMEM((n, LANES, LANES), F32)
    iscr = pltpu.VMEM((LANES * ROW_PITCH, LANES), F32)
    return pl.pallas_call(
        functools.partial(_scan_kernel, n=n, heads=heads, nb=nb),
        grid=(t // LANES,),
        in_specs=[tile] * 6 + [const(rk_c), const(g_c), const(be_c)],
        out_specs=[tile,
                   pl.BlockSpec((n, half, LANES), lambda ti: (0, 0, 0))],
        out_shape=[jax.ShapeDtypeStruct((nb, width, t), F32),
                   jax.ShapeDtypeStruct((n, half, LANES), F32)],
        scratch_shapes=[jscr] * 5 + [iscr, pltpu.VMEM((half, LANES), F32)],
        compiler_params=_cparams(("arbitrary",), SCAN_VMEM_LIMIT),
        name="rwkv_scan",
    )(d_t, b_t, k_t, r_t, kk_t, v_t, rk_c, g_c, be_c)


def _back_kernel(x_ref, px_ref, yt_ref, gate_ref, wpx_ref, wrw_ref, fg_ref, o_ref, *, final):
    yg = (yt_ref[...] * gate_ref[...]).T.astype(BF16)
    out = x_ref[...] + jnp.dot(px_ref[...], wpx_ref[...], preferred_element_type=F32)
    out = out + jnp.dot(yg, wrw_ref[...], preferred_element_type=F32)
    if final:
        out = _rms(out, fg_ref[...])
    o_ref[...] = out


def _prompt_back(x, px, y_t, gate_t, w_px, w_rw, fin_g, tt, final):
    b, t, d = x.shape
    width = y_t.shape[1]
    fin_g = fin_g.reshape(1, d)
    const = lambda a: pl.BlockSpec(a.shape, lambda bi, ti: (0,) * a.ndim)
    tile = lambda c: pl.BlockSpec((None, tt, c), lambda bi, ti: (bi, ti, 0))
    cmaj = pl.BlockSpec((None, width, tt), lambda bi, ti: (bi, 0, ti))
    return pl.pallas_call(
        functools.partial(_back_kernel, final=final),
        grid=(b, t // tt),
        in_specs=[tile(d), tile(px.shape[2]), cmaj, cmaj, const(w_px), const(w_rw), const(fin_g)],
        out_specs=tile(d),
        out_shape=jax.ShapeDtypeStruct((b, t, d), F32),
        compiler_params=_cparams(("arbitrary", "arbitrary")),
        name="back",
    )(x, px, y_t, gate_t, w_px, w_rw, fin_g)


def _rwkv_vectors(rws, w0, w_w2, a0, w_a2, k_k, k_a, width):
    r = rws[:, :width]
    k = rws[:, width:2 * width]
    v = rws[:, 2 * width:3 * width]
    wd = rws[:, 3 * width:3 * width + DECAY_LORA]
    ad = rws[:, 3 * width + DECAY_LORA:]
    wlin = w0 + jnp.dot(jnp.tanh(wd), w_w2, precision=HIGHEST, preferred_element_type=F32)
    decay = jnp.exp(-jnp.exp(-_softplus(-wlin) - 0.5))
    a = _sigmoid(a0 + jnp.dot(ad, w_a2, precision=HIGHEST, preferred_element_type=F32))
    kk_raw = k * k_k
    k = k * (1.0 + (a - 1.0) * k_a)
    return r, k, v, decay, kk_raw, a


def _sample_kernel(pv_ref, pg_ref, rw_ref, rg_ref, q_ref, xg_ref, pbuf_ref, sprev_ref, s_ref,
                   ck_ref, cv_ref, wbd_ref, ps_ref, mu_ref, w0_ref, ww2_ref, a0_ref, wa2_ref,
                   kk_ref, ka_ref, rk_ref, g_ref, be_ref,
                   mixed_ref, so_ref, y_scr, xa_scr, *, bt, start_pos):
    n = HEAD_DIM
    pv = pv_ref[...]
    pw = pv.shape[1]
    sums = []
    acc = pv
    for back in range(1, max(POOL_WINDOWS)):
        acc = acc + pbuf_ref[POOL_BUF - back]
        if back + 1 in POOL_WINDOWS:
            sums.append(acc)
    group = lax.broadcasted_iota(jnp.int32, (bt, pw), 1) // (pw // len(POOL_WINDOWS))
    cnts = [jnp.full((bt, pw), float(min(start_pos + 1, w)), F32) for w in POOL_WINDOWS]
    pooled = _pool_select(group, sums) / _pool_select(group, cnts) - pv
    pool_y = jnp.dot(pooled, wbd_ref[...], precision=HIGHEST, preferred_element_type=F32)
    pool_y = pool_y * ps_ref[...] * _silu(pg_ref[...])

    rw = rw_ref[...]
    rws = rw + (sprev_ref[...] - rw) * mu_ref[...]
    width = w0_ref.shape[1]
    heads = width // n
    r, k, v, decay, kk_raw, a = _rwkv_vectors(
        rws, w0_ref[...], ww2_ref[...], a0_ref[...], wa2_ref[...], kk_ref[...], ka_ref[...], width)
    eye = (lax.broadcasted_iota(jnp.int32, (n, n), 0) == lax.broadcasted_iota(jnp.int32, (n, n), 1))[None]
    for h in range(heads):
        sl = slice(h * n, (h + 1) * n)
        r_h, k_h, v_h = r[:, sl], k[:, sl], v[:, sl]
        kk_h = kk_raw[:, sl]
        kk_h = kk_h / jnp.maximum(jnp.sqrt(jnp.sum(kk_h * kk_h, axis=-1, keepdims=True)), 1e-12)
        b_h = kk_h * a[:, sl]
        s = s_ref[:, h]
        s_kk = jnp.sum(s * kk_h[:, None, :], axis=-1, keepdims=True)
        v_col = jnp.sum(jnp.where(eye, v_h[:, None, :], 0.0), axis=-1, keepdims=True)
        s = s * decay[:, sl][:, None, :] - s_kk * b_h[:, None, :] + v_col * k_h[:, None, :]
        so_ref[:, h] = s
        y_col = jnp.sum(s * r_h[:, None, :], axis=-1, keepdims=True)
        y = jnp.sum(jnp.where(eye, y_col, 0.0), axis=1)
        d = y - jnp.mean(y, axis=-1, keepdims=True)
        var = jnp.mean(d * d, axis=-1, keepdims=True)
        y = d * lax.rsqrt(var + GN_EPS) * g_ref[:, sl] + be_ref[:, sl]
        bonus = jnp.sum(r_h * k_h * rk_ref[:, sl], axis=-1, keepdims=True)
        y_scr[:, sl] = y + bonus * v_h
    rwkv_y = y_scr[...] * _silu(rg_ref[...])

    q = q_ref[...]
    xw = q.shape[1]
    row_h = lax.broadcasted_iota(jnp.int32, (SUBLANES, xw), 0)
    lane_h = lax.broadcasted_iota(jnp.int32, (SUBLANES, xw), 1) // (xw // XA_HEADS)
    for bi in range(bt):
        qm = jnp.where(row_h == lane_h, q[bi:bi + 1, :], 0.0)
        o8 = _cross_attend_rows(qm, ck_ref[bi], cv_ref[bi])
        xa_scr[bi:bi + 1, :] = jnp.sum(jnp.where(row_h == lane_h, o8, 0.0), axis=0, keepdims=True)
    xa_y = xa_scr[...] * _silu(xg_ref[...])

    mixed_ref[:, 0:pw] = pool_y
    mixed_ref[:, pw:pw + width] = rwkv_y
    mixed_ref[:, pw + width:] = xa_y


def _sample_mix(pv, pg, rw, rg, q, xg, pbuf_t, sprev, s0, ck, cv, wbd, vecs, ww2, wa2, bt, start_pos):
    b = pv.shape[0]
    row = lambda c: pl.BlockSpec((bt, c), lambda i: (i, 0))
    const = lambda a: pl.BlockSpec(a.shape, lambda i: (0,) * a.ndim)
    lead = lambda a: pl.BlockSpec((bt,) + a.shape[1:], lambda i: (i,) + (0,) * (a.ndim - 1))
    pscale, mu, w0, a0, k_k, k_a, r_k, ln_g, ln_b = [a.reshape(1, -1) for a in vecs]
    mixw = pv.shape[1] + w0.shape[1] + q.shape[1]
    return pl.pallas_call(
        functools.partial(_sample_kernel, bt=bt, start_pos=start_pos),
        grid=(b // bt,),
        in_specs=[row(pv.shape[1]), row(pg.shape[1]), row(rw.shape[1]), row(rg.shape[1]),
                  row(q.shape[1]), row(xg.shape[1]),
                  pl.BlockSpec((POOL_BUF, bt, pv.shape[1]), lambda i: (0, i, 0)),
                  row(sprev.shape[1]), lead(s0), lead(ck), lead(cv),
                  const(wbd), const(pscale), const(mu), const(w0), const(ww2), const(a0), const(wa2),
                  const(k_k), const(k_a), const(r_k), const(ln_g), const(ln_b)],
        out_specs=[row(mixw), lead(s0)],
        out_shape=[jax.ShapeDtypeStruct((b, mixw), F32), jax.ShapeDtypeStruct(s0.shape, F32)],
        scratch_shapes=[pltpu.VMEM((bt, w0.shape[1]), F32), pltpu.VMEM((bt, q.shape[1]), F32)],
        compiler_params=_cparams(("arbitrary",)),
        name="sample_mix",
    )(pv, pg, rw, rg, q, xg, pbuf_t, sprev, s0, ck, cv, wbd, pscale, mu, w0, ww2, a0, wa2,
      k_k, k_a, r_k, ln_g, ln_b)


def _out_kernel(x_ref, m_ref, wo_ref, fg_ref, o_ref, *, final):
    out = x_ref[...] + jnp.dot(m_ref[...].astype(BF16), wo_ref[...], preferred_element_type=F32)
    if final:
        out = _rms(out, fg_ref[...])
    o_ref[...] = out


def _out_project(x2d, mixed, wo_bf16, fin_g, final):
    m, d = x2d.shape
    full = lambda a: pl.BlockSpec(a.shape, lambda i: (0,) * a.ndim)
    fin_g = fin_g.reshape(1, -1)
    return pl.pallas_call(
        functools.partial(_out_kernel, final=final),
        grid=(1,),
        in_specs=[full(x2d), full(mixed), full(wo_bf16), full(fin_g)],
        out_specs=full(x2d),
        out_shape=jax.ShapeDtypeStruct((m, d), F32),
        compiler_params=_cparams(("arbitrary",)),
        name="out_proj",
    )(x2d, mixed, wo_bf16, fin_g)


def _block_diag(pool_w):
    g_n, g, _ = pool_w.shape
    eye = jnp.eye(g_n, dtype=pool_w.dtype)
    return (eye[:, None, :, None] * pool_w[:, :, None, :]).reshape(g_n * g, g_n * g)


def _pick_tile(n, pref):
    t = min(n, pref)
    while n % t:
        t //= 2
    return t


def _channel_orders(heads):
    n = HEAD_DIM
    j = np.arange(n)[:, None]
    h = np.arange(heads)[None, :]
    perm_j = (h * n + j).reshape(-1)
    il = np.arange(n // 2)[:, None, None]
    ih = np.arange(2)[None, :, None]
    h3 = np.arange(heads)[None, None, :]
    perm_i = (h3 * n + ih * (n // 2) + il).reshape(-1)
    return perm_j, perm_i


def kernel(x_prompt, x_sample, mem_prompt, state_pool, state_shift, state_wkv, cache_mem_k, cache_mem_v, norm_g, w_in, w_out, pool_w, pool_scale, shift_mu, w0, w_w2, a0, w_a2, k_k, k_a, r_k, ln_x_g, ln_x_b, mem_norm_g, w_kv, final_norm_g):
    depth = w_in.shape[0]
    b, t, d = x_prompt.shape
    db = x_sample.shape[0]
    n_mem = mem_prompt.shape[1]
    pw = pool_scale.shape[1]
    width = w0.shape[1]
    heads = width // HEAD_DIM
    sw = shift_mu.shape[1]
    xw = w_kv.shape[2] // 2
    half = HEAD_DIM // 2
    assert 2 * b * heads == LANES, "the recurrence kernel places the (batch, head) chains on the lanes twice"
    assert t % LANES == 0
    splits = (pw, pw, sw, width, xw, xw)
    start_pos = 16384

    w_in_b = w_in.astype(BF16)
    w_out_b = w_out.astype(BF16)
    w_kv_b = w_kv.astype(BF16)

    perm_j, perm_i = _channel_orders(heads)
    rw_perm = np.concatenate([perm_j, width + perm_j, 2 * width + perm_i, 3 * width + np.arange(sw - 3 * width)])
    rw_inv = np.argsort(rw_perm)
    rw0 = 2 * pw
    rg0 = rw0 + sw
    t_cols = np.concatenate([rw0 + rw_perm, rg0 + perm_i])
    n_cols = np.concatenate([np.arange(0, rw0), np.arange(rg0 + width, rg0 + width + 2 * xw)])
    col = lambda a: a.reshape(-1, 1)

    tt = _pick_tile(t, 256)
    bt = _pick_tile(db, 8)

    mem2d = mem_prompt.reshape(b * n_mem, d)
    memk, memv = [], []
    for l in range(depth):
        mk, mv = _project(mem2d, mem_norm_g[l], w_kv_b[l], (xw, xw), _pick_tile(b * n_mem, 512))
        memk.append(mk.reshape(b, n_mem, xw))
        memv.append(mv.reshape(b, n_mem, xw))

    x = x_prompt
    pool_p, shift_p, wkv_p = [], [], []
    for l in range(depth):
        w_t = w_in_b[l][:, t_cols].T
        w_nat = w_in_b[l][:, n_cols]
        cols = (col(shift_mu[l][rw_perm]), col(w0[l][perm_j]), col(a0[l][perm_j]),
                col(k_k[l][perm_j]), col(k_a[l][perm_j]))
        outs = _prompt_front(x, norm_g[l], w_nat, w_t, cols, w_w2[l][:, perm_j].T, w_a2[l][:, perm_j].T,
                             memk[l], memv[l], _block_diag(pool_w[l]), pool_scale[l], tt, width, heads)
        d_t, b_t, k_t, r_t, kk_t, v_t, gate_t, px, ptail, shift_t = outs
        pool_p.append(ptail[:, POOL_HALO - POOL_BUF:])
        shift_p.append(shift_t[:, :, LANES - 1][:, rw_inv])
        gn = lambda p: p[perm_i].reshape(HEAD_DIM, heads, 1)
        y_t, s_c = _rwkv_scan(d_t, b_t, k_t, r_t, kk_t, v_t, col(r_k[l].reshape(-1)[perm_j]),
                              gn(ln_x_g[l]), gn(ln_x_b[l]), heads)
        wkv_p.append(s_c.reshape(HEAD_DIM, half, 2, b, heads).transpose(3, 4, 2, 1, 0)
                     .reshape(b, heads, HEAD_DIM, HEAD_DIM))
        w_px = jnp.concatenate([w_out_b[l][0:pw], w_out_b[l][pw + width:]], axis=0)
        w_rw = w_out_b[l][pw:pw + width][perm_i]
        x = _prompt_back(x, px, y_t, gate_t, w_px, w_rw, final_norm_g, tt, l == depth - 1)
    y_prompt = x

    xs = x_sample.reshape(db, d)
    pool_s, shift_s, wkv_s = [], [], []
    for l in range(depth):
        pv, pg, rw, rg, q, xg = _project(xs, norm_g[l], w_in_b[l], splits, _pick_tile(db, 128))
        pool_s.append(jnp.concatenate([state_pool[l][:, 1:], pv[:, None, :]], axis=1))
        shift_s.append(rw)
        vecs = (pool_scale[l], shift_mu[l], w0[l], a0[l], k_k[l], k_a[l], r_k[l], ln_x_g[l], ln_x_b[l])
        mixed, s_new = _sample_mix(
            pv, pg, rw, rg, q, xg, state_pool[l].transpose(1, 0, 2), state_shift[l], state_wkv[l],
            cache_mem_k[l].reshape(db, n_mem, xw), cache_mem_v[l].reshape(db, n_mem, xw),
            _block_diag(pool_w[l]), vecs, w_w2[l], w_a2[l], bt, start_pos)
        wkv_s.append(s_new)
        xs = _out_project(xs, mixed, w_out_b[l], final_norm_g, l == depth - 1)
    y_sample = xs.reshape(db, 1, d)

    kv_shape = (depth, b, n_mem, XA_HEADS, xw // XA_HEADS)
    return (y_prompt, y_sample, jnp.stack(pool_p), jnp.stack(shift_p), jnp.stack(wkv_p),
            jnp.stack(memk).reshape(kv_shape), jnp.stack(memv).reshape(kv_shape),
            jnp.stack(pool_s), jnp.stack(shift_s), jnp.stack(wkv_s))
```

```python
import functools

import numpy as np
import jax
import jax.numpy as jnp
from jax import lax
from jax.experimental import pallas as pl
from jax.experimental.pallas import tpu as pltpu

F32 = jnp.float32
BF16 = jnp.bfloat16
HIGHEST = lax.Precision.HIGHEST

HEAD_DIM = 64
POOL_WINDOWS = (2, 4, 8, 16)
POOL_BUF = max(POOL_WINDOWS) - 1
POOL_HALO = 16
XA_HEADS = 4
DECAY_LORA = 64
ICLR_LORA = 64
EPS = 1e-6
GN_EPS = HEAD_DIM * 1e-5
LANES = 128
SUBLANES = 8
ROW_PITCH = 40
VMEM_LIMIT = 48 * 1024 * 1024
SCAN_VMEM_LIMIT = 60 * 1024 * 1024


def _cparams(sem, limit=VMEM_LIMIT):
    return pltpu.CompilerParams(dimension_semantics=sem, vmem_limit_bytes=limit)


def _silu(x):
    return x / (1.0 + jnp.exp(-x))


def _sigmoid(x):
    return 1.0 / (1.0 + jnp.exp(-x))


def _softplus(x):
    return jnp.maximum(x, 0.0) + jnp.log1p(jnp.exp(-jnp.abs(x)))


def _split_bf16(x):
    hi = x.astype(BF16)
    return hi, (x - hi.astype(F32)).astype(BF16)


def _dot3(a, b):
    a_hi, a_lo = _split_bf16(a)
    b_hi, b_lo = _split_bf16(b)
    dot = functools.partial(jnp.dot, preferred_element_type=F32)
    return dot(a_hi, b_hi) + (dot(a_hi, b_lo) + dot(a_lo, b_hi))


def _rms(x, g):
    return x * lax.rsqrt(jnp.mean(x * x, axis=-1, keepdims=True) + EPS) * g


def _proj_kernel(x_ref, g_ref, w_ref, *out_refs, splits):
    xn = _rms(x_ref[...], g_ref[...]).astype(BF16)
    off = 0
    for o_ref, n in zip(out_refs, splits):
        o_ref[...] = jnp.dot(xn, w_ref[:, off:off + n], preferred_element_type=F32)
        off += n


def _project(x2d, g, w_bf16, splits, tm):
    m, d = x2d.shape
    n = w_bf16.shape[1]
    return pl.pallas_call(
        functools.partial(_proj_kernel, splits=splits),
        grid=(m // tm,),
        in_specs=[pl.BlockSpec((tm, d), lambda i: (i, 0)),
                  pl.BlockSpec((1, d), lambda i: (0, 0)),
                  pl.BlockSpec((d, n), lambda i: (0, 0))],
        out_specs=[pl.BlockSpec((tm, s), lambda i: (i, 0)) for s in splits],
        out_shape=[jax.ShapeDtypeStruct((m, s), F32) for s in splits],
        compiler_params=_cparams(("arbitrary",)),
        name="proj",
    )(x2d, g.reshape(1, d), w_bf16)


def _pool_select(lane_group, vals):
    out = vals[-1]
    for gi in range(len(vals) - 2, -1, -1):
        out = jnp.where(lane_group == gi, vals[gi], out)
    return out


def _cross_attend_rows(qm, mk, mv):
    hd = mk.shape[1] // XA_HEADS
    s = lax.dot_general(qm, mk, (((1,), (1,)), ((), ())), preferred_element_type=F32) * (hd ** -0.5)
    p = jnp.exp(s - jnp.max(s, axis=-1, keepdims=True))
    p = p / jnp.sum(p, axis=-1, keepdims=True)
    return jnp.dot(p.astype(mv.dtype), mv, preferred_element_type=F32)


def _front_kernel(x_ref, g_ref, wn_ref, wt_ref, mu_ref, w0_ref, ww2_ref, a0_ref, wa2_ref, kk_ref, ka_ref,
                  mk_ref, mv_ref, wbd_ref, ps_ref,
                  d_o, b_o, k_o, r_o, kkn_o, v_o, gate_o, px_o, ptail_o, shift_o,
                  ext, carry, *, tt, width, heads):
    ti = pl.program_id(1)

    @pl.when(ti == 0)
    def _():
        ext[0:POOL_HALO, :] = jnp.zeros((POOL_HALO, ext.shape[1]), F32)
        carry[...] = jnp.zeros_like(carry)

    xn = _rms(x_ref[...], g_ref[...]).astype(BF16)
    nat = jnp.dot(xn, wn_ref[...], preferred_element_type=F32)
    cm = lax.dot_general(wt_ref[...], xn, (((1,), (1,)), ((), ())), preferred_element_type=F32)
    sw = mu_ref.shape[0]
    rwt = cm[0:sw]
    gate_o[...] = _silu(cm[sw:])

    prev = pltpu.roll(jnp.concatenate([carry[...], rwt], axis=1), 1, 1)[:, LANES:]
    carry[...] = rwt[:, tt - LANES:]
    shift_o[...] = rwt[:, tt - LANES:]
    rws = rwt + (prev - rwt) * mu_ref[...]
    r = rws[0:width]
    k = rws[width:2 * width]
    v = rws[2 * width:3 * width]
    wd = rws[3 * width:3 * width + DECAY_LORA]
    ad = rws[3 * width + DECAY_LORA:]
    wlin = w0_ref[...] + _dot3(ww2_ref[...], jnp.tanh(wd))
    decay = jnp.exp(-jnp.exp(-_softplus(-wlin) - 0.5))
    a = _sigmoid(a0_ref[...] + _dot3(wa2_ref[...], ad))
    kk3 = (k * kk_ref[...]).reshape(HEAD_DIM, heads, tt)
    nrm = jnp.maximum(jnp.sqrt(jnp.sum(kk3 * kk3, axis=0, keepdims=True)), 1e-12)
    kkn = (kk3 / nrm).reshape(width, tt)
    d_o[...] = decay
    b_o[...] = kkn * a
    k_o[...] = k * (1.0 + (a - 1.0) * ka_ref[...])
    r_o[...] = r
    kkn_o[...] = kkn
    v_o[...] = v

    pw = ps_ref.shape[1]
    xw = mk_ref.shape[1]
    pv = nat[:, 0:pw]
    pg = nat[:, pw:2 * pw]
    q = nat[:, 2 * pw:2 * pw + xw]
    xg = nat[:, 2 * pw + xw:]

    ext[POOL_HALO:POOL_HALO + tt, :] = pv
    sums = []
    acc = pv
    for back in range(1, max(POOL_WINDOWS)):
        acc = acc + ext[POOL_HALO - back:POOL_HALO - back + tt, :]
        if back + 1 in POOL_WINDOWS:
            sums.append(acc)
    tail = ext[tt:tt + POOL_HALO, :]
    ext[0:POOL_HALO, :] = tail
    ptail_o[...] = tail
    group = lax.broadcasted_iota(jnp.int32, (tt, pw), 1) // (pw // len(POOL_WINDOWS))
    pos = ti * tt + lax.broadcasted_iota(jnp.int32, (tt, pw), 0)
    win = _pool_select(group, [jnp.full((tt, pw), w, jnp.int32) for w in POOL_WINDOWS])
    cnt = jnp.minimum(pos + 1, win).astype(F32)
    pooled = _pool_select(group, sums) / cnt - pv
    pool_y = _dot3(pooled, wbd_ref[...])
    px_o[:, 0:pw] = (pool_y * ps_ref[...] * _silu(pg)).astype(BF16)

    mk = mk_ref[...].astype(BF16)
    mv = mv_ref[...].astype(BF16)
    head = lax.broadcasted_iota(jnp.int32, (tt, xw), 1) // (xw // XA_HEADS)
    xa = jnp.zeros((tt, xw), F32)
    for h in range(XA_HEADS):
        qm = jnp.where(head == h, q, 0.0).astype(BF16)
        xa = jnp.where(head == h, _cross_attend_rows(qm, mk, mv), xa)
    px_o[:, pw:] = (xa * _silu(xg)).astype(BF16)


def _prompt_front(x, g, w_nat, w_t, cols, ww2_t, wa2_t, mk, mv, wbd, pscale, tt, width, heads):
    b, t, d = x.shape
    mu_c, w0_c, a0_c, kk_c, ka_c = cols
    sw = mu_c.shape[0]
    pw = pscale.shape[-1]
    xw = mk.shape[2]
    g = g.reshape(1, d)
    pscale = pscale.reshape(1, pw)
    const = lambda a: pl.BlockSpec(a.shape, lambda bi, ti: (0,) * a.ndim)
    cmaj = pl.BlockSpec((None, width, tt), lambda bi, ti: (bi, 0, ti))
    perb = lambda s: pl.BlockSpec((None,) + s, lambda bi, ti: (bi, 0, 0))
    return pl.pallas_call(
        functools.partial(_front_kernel, tt=tt, width=width, heads=heads),
        grid=(b, t // tt),
        in_specs=[pl.BlockSpec((None, tt, d), lambda bi, ti: (bi, ti, 0)), const(g), const(w_nat), const(w_t),
                  const(mu_c), const(w0_c), const(ww2_t), const(a0_c), const(wa2_t), const(kk_c), const(ka_c),
                  perb(mk.shape[1:]), perb(mv.shape[1:]), const(wbd), const(pscale)],
        out_specs=[cmaj] * 7 + [pl.BlockSpec((None, tt, pw + xw), lambda bi, ti: (bi, ti, 0)),
                                perb((POOL_HALO, pw)), perb((sw, LANES))],
        out_shape=[jax.ShapeDtypeStruct((b, width, t), F32)] * 7
        + [jax.ShapeDtypeStruct((b, t, pw + xw), BF16),
           jax.ShapeDtypeStruct((b, POOL_HALO, pw), F32),
           jax.ShapeDtypeStruct((b, sw, LANES), F32)],
        scratch_shapes=[pltpu.VMEM((tt + POOL_HALO, pw), F32), pltpu.VMEM((sw, LANES), F32)],
        compiler_params=_cparams(("arbitrary", "arbitrary")),
        name="front",
    )(x, g, w_nat, w_t, mu_c, w0_c, ww2_t, a0_c, wa2_t, kk_c, ka_c, mk, mv, wbd, pscale)


def _scan_kernel(d_ref, b_ref, k_ref, r_ref, kk_ref, v_ref, rk_ref, g_ref, be_ref,
                 y_ref, s_ref, wj, bj, kj, rj, kkj, vy, u_scr, *, n, heads, nb):
    half = n // 2
    chains = nb * heads

    @pl.when(pl.program_id(0) == 0)
    def _():
        s_ref[...] = jnp.zeros_like(s_ref)

    def relayout_j(j, carry):
        row0 = pl.multiple_of(j * heads, SUBLANES)
        for src, dst in ((d_ref, wj), (b_ref, bj), (k_ref, kj), (r_ref, rj), (kk_ref, kkj)):
            m = src[:, pl.ds(row0, heads), :].reshape(chains, LANES)
            dst[j] = jnp.concatenate([m, m], axis=0).T
        return carry

    lax.fori_loop(0, n, relayout_j, 0)
    for il in range(half):
        lo = v_ref[:, (2 * il) * heads:(2 * il + 1) * heads, :].reshape(chains, LANES)
        hi = v_ref[:, (2 * il + 1) * heads:(2 * il + 2) * heads, :].reshape(chains, LANES)
        vy[pl.ds(il, LANES, stride=ROW_PITCH), :] = jnp.concatenate([lo, hi], axis=0).T

    acc = jnp.zeros((half, LANES), F32)
    for j in range(n):
        acc = acc + s_ref[j] * kkj[j, 0:1, :]
    u_scr[...] = acc

    def step(t, carry):
        tn = jnp.minimum(t + 1, LANES - 1)
        row = pl.multiple_of(t * ROW_PITCH, SUBLANES)
        u = u_scr[...]
        v = vy[pl.ds(row, half), :]
        y = jnp.zeros((half, LANES), F32)
        nacc = jnp.zeros((half, LANES), F32)
        for j in range(n):
            s = (s_ref[j] * wj[j, pl.ds(t, 1), :] - u * bj[j, pl.ds(t, 1), :]
                 + v * kj[j, pl.ds(t, 1), :])
            s_ref[j] = s
            y = y + s * rj[j, pl.ds(t, 1), :]
            nacc = nacc + s * kkj[j, pl.ds(tn, 1), :]
        u_scr[...] = nacc
        vy[pl.ds(row, half), :] = y
        return carry

    lax.fori_loop(0, LANES, step, 0)

    for il in range(half):
        yt = vy[pl.ds(il, LANES, stride=ROW_PITCH), :].T
        for ih in range(2):
            for bi in range(nb):
                src0 = (ih * nb + bi) * heads
                dst0 = (2 * il + ih) * heads
                y_ref[bi, dst0:dst0 + heads, :] = yt[src0:src0 + heads, :]
    for bi in range(nb):
        y3 = y_ref[bi].reshape(n, heads, LANES)
        d = y3 - jnp.sum(y3, axis=0, keepdims=True) * (1.0 / n)
        var = jnp.sum(d * d, axis=0, keepdims=True) * (1.0 / n)
        bonus = jnp.sum((r_ref[bi] * k_ref[bi] * rk_ref[...]).reshape(n, heads, LANES), axis=0, keepdims=True)
        out = d * lax.rsqrt(var + GN_EPS) * g_ref[...] + be_ref[...] + bonus * v_ref[bi].reshape(n, heads, LANES)
        y_ref[bi] = out.reshape(n * heads, LANES)


def _rwkv_scan(d_t, b_t, k_t, r_t, kk_t, v_t, rk_c, g_c, be_c, heads):
    nb, width, t = d_t.shape
    n = width // heads
    half = n // 2
    tile = pl.BlockSpec((nb, width, LANES), lambda ti: (0, 0, ti))
    const = lambda a: pl.BlockSpec(a.shape, lambda ti: (0,) * a.ndim)
    jscr = pltpu.VMEM((n, LANES, LANES), F32)
    iscr = pltpu.VMEM((LANES * ROW_PITCH, LANES), F32)
    return pl.pallas_call(
        functools.partial(_scan_kernel, n=n, heads=heads, nb=nb),
        grid=(t // LANES,),
        in_specs=[tile] * 6 + [const(rk_c), const(g_c), const(be_c)],
        out_specs=[tile,
                   pl.BlockSpec((n, half, LANES), lambda ti: (0, 0, 0))],
        out_shape=[jax.ShapeDtypeStruct((nb, width, t), F32),
                   jax.ShapeDtypeStruct((n, half, LANES), F32)],
        scratch_shapes=[jscr] * 5 + [iscr, pltpu.VMEM((half, LANES), F32)],
        compiler_params=_cparams(("arbitrary",), SCAN_VMEM_LIMIT),
        name="rwkv_scan",
    )(d_t, b_t, k_t, r_t, kk_t, v_t, rk_c, g_c, be_c)


def _back_kernel(x_ref, px_ref, yt_ref, gate_ref, wpx_ref, wrw_ref, fg_ref, o_ref, *, final):
    yg = (yt_ref[...] * gate_ref[...]).T.astype(BF16)
    out = x_ref[...] + jnp.dot(px_ref[...], wpx_ref[...], preferred_element_type=F32)
    out = out + jnp.dot(yg, wrw_ref[...], preferred_element_type=F32)
    if final:
        out = _rms(out, fg_ref[...])
    o_ref[...] = out


def _prompt_back(x, px, y_t, gate_t, w_px, w_rw, fin_g, tt, final):
    b, t, d = x.shape
    width = y_t.shape[1]
    fin_g = fin_g.reshape(1, d)
    const = lambda a: pl.BlockSpec(a.shape, lambda bi, ti: (0,) * a.ndim)
    tile = lambda c: pl.BlockSpec((None, tt, c), lambda bi, ti: (bi, ti, 0))
    cmaj = pl.BlockSpec((None, width, tt), lambda bi, ti: (bi, 0, ti))
    return pl.pallas_call(
        functools.partial(_back_kernel, final=final),
        grid=(b, t // tt),
        in_specs=[tile(d), tile(px.shape[2]), cmaj, cmaj, const(w_px), const(w_rw), const(fin_g)],
        out_specs=tile(d),
        out_shape=jax.ShapeDtypeStruct((b, t, d), F32),
        compiler_params=_cparams(("arbitrary", "arbitrary")),
        name="back",
    )(x, px, y_t, gate_t, w_px, w_rw, fin_g)


def _rwkv_vectors(rws, w0, w_w2, a0, w_a2, k_k, k_a, width):
    r = rws[:, :width]
    k = rws[:, width:2 * width]
    v = rws[:, 2 * width:3 * width]
    wd = rws[:, 3 * width:3 * width + DECAY_LORA]
    ad = rws[:, 3 * width + DECAY_LORA:]
    wlin = w0 + jnp.dot(jnp.tanh(wd), w_w2, precision=HIGHEST, preferred_element_type=F32)
    decay = jnp.exp(-jnp.exp(-_softplus(-wlin) - 0.5))
    a = _sigmoid(a0 + jnp.dot(ad, w_a2, precision=HIGHEST, preferred_element_type=F32))
    kk_raw = k * k_k
    k = k * (1.0 + (a - 1.0) * k_a)
    return r, k, v, decay, kk_raw, a


def _sample_kernel(pv_ref, pg_ref, rw_ref, rg_ref, q_ref, xg_ref, pbuf_ref, sprev_ref, s_ref,
                   ck_ref, cv_ref, wbd_ref, ps_ref, mu_ref, w0_ref, ww2_ref, a0_ref, wa2_ref,
                   kk_ref, ka_ref, rk_ref, g_ref, be_ref,
                   mixed_ref, so_ref, y_scr, xa_scr, *, bt, start_pos):
    n = HEAD_DIM
    pv = pv_ref[...]
    pw = pv.shape[1]
    sums = []
    acc = pv
    for back in range(1, max(POOL_WINDOWS)):
        acc = acc + pbuf_ref[POOL_BUF - back]
        if back + 1 in POOL_WINDOWS:
            sums.append(acc)
    group = lax.broadcasted_iota(jnp.int32, (bt, pw), 1) // (pw // len(POOL_WINDOWS))
    cnts = [jnp.full((bt, pw), float(min(start_pos + 1, w)), F32) for w in POOL_WINDOWS]
    pooled = _pool_select(group, sums) / _pool_select(group, cnts) - pv
    pool_y = jnp.dot(pooled, wbd_ref[...], precision=HIGHEST, preferred_element_type=F32)
    pool_y = pool_y * ps_ref[...] * _silu(pg_ref[...])

    rw = rw_ref[...]
    rws = rw + (sprev_ref[...] - rw) * mu_ref[...]
    width = w0_ref.shape[1]
    heads = width // n
    r, k, v, decay, kk_raw, a = _rwkv_vectors(
        rws, w0_ref[...], ww2_ref[...], a0_ref[...], wa2_ref[...], kk_ref[...], ka_ref[...], width)
    eye = (lax.broadcasted_iota(jnp.int32, (n, n), 0) == lax.broadcasted_iota(jnp.int32, (n, n), 1))[None]
    for h in range(heads):
        sl = slice(h * n, (h + 1) * n)
        r_h, k_h, v_h = r[:, sl], k[:, sl], v[:, sl]
        kk_h = kk_raw[:, sl]
        kk_h = kk_h / jnp.maximum(jnp.sqrt(jnp.sum(kk_h * kk_h, axis=-1, keepdims=True)), 1e-12)
        b_h = kk_h * a[:, sl]
        s = s_ref[:, h]
        s_kk = jnp.sum(s * kk_h[:, None, :], axis=-1, keepdims=True)
        v_col = jnp.sum(jnp.where(eye, v_h[:, None, :], 0.0), axis=-1, keepdims=True)
        s = s * decay[:, sl][:, None, :] - s_kk * b_h[:, None, :] + v_col * k_h[:, None, :]
        so_ref[:, h] = s
        y_col = jnp.sum(s * r_h[:, None, :], axis=-1, keepdims=True)
        y = jnp.sum(jnp.where(eye, y_col, 0.0), axis=1)
        d = y - jnp.mean(y, axis=-1, keepdims=True)
        var = jnp.mean(d * d, axis=-1, keepdims=True)
        y = d * lax.rsqrt(var + GN_EPS) * g_ref[:, sl] + be_ref[:, sl]
        bonus = jnp.sum(r_h * k_h * rk_ref[:, sl], axis=-1, keepdims=True)
        y_scr[:, sl] = y + bonus * v_h
    rwkv_y = y_scr[...] * _silu(rg_ref[...])

    q = q_ref[...]
    xw = q.shape[1]
    row_h = lax.broadcasted_iota(jnp.int32, (SUBLANES, xw), 0)
    lane_h = lax.broadcasted_iota(jnp.int32, (SUBLANES, xw), 1) // (xw // XA_HEADS)
    for bi in range(bt):
        qm = jnp.where(row_h == lane_h, q[bi:bi + 1, :], 0.0)
        o8 = _cross_attend_rows(qm, ck_ref[bi], cv_ref[bi])
        xa_scr[bi:bi + 1, :] = jnp.sum(jnp.where(row_h == lane_h, o8, 0.0), axis=0, keepdims=True)
    xa_y = xa_scr[...] * _silu(xg_ref[...])

    mixed_ref[:, 0:pw] = pool_y
    mixed_ref[:, pw:pw + width] = rwkv_y
    mixed_ref[:, pw + width:] = xa_y


def _sample_mix(pv, pg, rw, rg, q, xg, pbuf_t, sprev, s0, ck, cv, wbd, vecs, ww2, wa2, bt, start_pos):
    b = pv.shape[0]
    row = lambda c: pl.BlockSpec((bt, c), lambda i: (i, 0))
    const = lambda a: pl.BlockSpec(a.shape, lambda i: (0,) * a.ndim)
    lead = lambda a: pl.BlockSpec((bt,) + a.shape[1:], lambda i: (i,) + (0,) * (a.ndim - 1))
    pscale, mu, w0, a0, k_k, k_a, r_k, ln_g, ln_b = [a.reshape(1, -1) for a in vecs]
    mixw = pv.shape[1] + w0.shape[1] + q.shape[1]
    return pl.pallas_call(
        functools.partial(_sample_kernel, bt=bt, start_pos=start_pos),
        grid=(b // bt,),
        in_specs=[row(pv.shape[1]), row(pg.shape[1]), row(rw.shape[1]), row(rg.shape[1]),
                  row(q.shape[1]), row(xg.shape[1]),
                  pl.BlockSpec((POOL_BUF, bt, pv.shape[1]), lambda i: (0, i, 0)),
                  row(sprev.shape[1]), lead(s0), lead(ck), lead(cv),
                  const(wbd), const(pscale), const(mu), const(w0), const(ww2), const(a0), const(wa2),
                  const(k_k), const(k_a), const(r_k), const(ln_g), const(ln_b)],
        out_specs=[row(mixw), lead(s0)],
        out_shape=[jax.ShapeDtypeStruct((b, mixw), F32), jax.ShapeDtypeStruct(s0.shape, F32)],
        scratch_shapes=[pltpu.VMEM((bt, w0.shape[1]), F32), pltpu.VMEM((bt, q.shape[1]), F32)],
        compiler_params=_cparams(("arbitrary",)),
        name="sample_mix",
    )(pv, pg, rw, rg, q, xg, pbuf_t, sprev, s0, ck, cv, wbd, pscale, mu, w0, ww2, a0, wa2,
      k_k, k_a, r_k, ln_g, ln_b)


def _out_kernel(x_ref, m_ref, wo_ref, fg_ref, o_ref, *, final):
    out = x_ref[...] + jnp.dot(m_ref[...].astype(BF16), wo_ref[...], preferred_element_type=F32)
    if final:
        out = _rms(out, fg_ref[...])
    o_ref[...] = out


def _out_project(x2d, mixed, wo_bf16, fin_g, final):
    m, d = x2d.shape
    full = lambda a: pl.BlockSpec(a.shape, lambda i: (0,) * a.ndim)
    fin_g = fin_g.reshape(1, -1)
    return pl.pallas_call(
        functools.partial(_out_kernel, final=final),
        grid=(1,),
        in_specs=[full(x2d), full(mixed), full(wo_bf16), full(fin_g)],
        out_specs=full(x2d),
        out_shape=jax.ShapeDtypeStruct((m, d), F32),
        compiler_params=_cparams(("arbitrary",)),
        name="out_proj",
    )(x2d, mixed, wo_bf16, fin_g)


def _block_diag(pool_w):
    g_n, g, _ = pool_w.shape
    eye = jnp.eye(g_n, dtype=pool_w.dtype)
    return (eye[:, None, :, None] * pool_w[:, :, None, :]).reshape(g_n * g, g_n * g)


def _pick_tile(n, pref):
    t = min(n, pref)
    while n % t:
        t //= 2
    return t


def _channel_orders(heads):
    n = HEAD_DIM
    j = np.arange(n)[:, None]
    h = np.arange(heads)[None, :]
    perm_j = (h * n + j).reshape(-1)
    il = np.arange(n // 2)[:, None, None]
    ih = np.arange(2)[None, :, None]
    h3 = np.arange(heads)[None, None, :]
    perm_i = (h3 * n + ih * (n // 2) + il).reshape(-1)
    return perm_j, perm_i


def kernel(x_prompt, x_sample, mem_prompt, state_pool, state_shift, state_wkv, cache_mem_k, cache_mem_v, norm_g, w_in, w_out, pool_w, pool_scale, shift_mu, w0, w_w2, a0, w_a2, k_k, k_a, r_k, ln_x_g, ln_x_b, mem_norm_g, w_kv, final_norm_g):
    depth = w_in.shape[0]
    b, t, d = x_prompt.shape
    db = x_sample.shape[0]
    n_mem = mem_prompt.shape[1]
    pw = pool_scale.shape[1]
    width = w0.shape[1]
    heads = width // HEAD_DIM
    sw = shift_mu.shape[1]
    xw = w_kv.shape[2] // 2
    half = HEAD_DIM // 2
    assert 2 * b * heads == LANES, "the recurrence kernel places the (batch, head) chains on the lanes twice"
    assert t % LANES == 0
    splits = (pw, pw, sw, width, xw, xw)
    start_pos = 16384

    w_in_b = w_in.astype(BF16)
    w_out_b = w_out.astype(BF16)
    w_kv_b = w_kv.astype(BF16)

    perm_j, perm_i = _channel_orders(heads)
    rw_perm = np.concatenate([perm_j, width + perm_j, 2 * width + perm_i, 3 * width + np.arange(sw - 3 * width)])
    rw_inv = np.argsort(rw_perm)
    rw0 = 2 * pw
    rg0 = rw0 + sw
    t_cols = np.concatenate([rw0 + rw_perm, rg0 + perm_i])
    n_cols = np.concatenate([np.arange(0, rw0), np.arange(rg0 + width, rg0 + width + 2 * xw)])
    col = lambda a: a.reshape(-1, 1)

    tt = _pick_tile(t, 512)
    bt = _pick_tile(db, 8)

    mem2d = mem_prompt.reshape(b * n_mem, d)
    memk, memv = [], []
    for l in range(depth):
        mk, mv = _project(mem2d, mem_norm_g[l], w_kv_b[l], (xw, xw), _pick_tile(b * n_mem, 512))
        memk.append(mk.reshape(b, n_mem, xw))
        memv.append(mv.reshape(b, n_mem, xw))

    x = x_prompt
    pool_p, shift_p, wkv_p = [], [], []
    for l in range(depth):
        w_t = w_in_b[l][:, t_cols].T
        w_nat = w_in_b[l][:, n_cols]
        cols = (col(shift_mu[l][rw_perm]), col(w0[l][perm_j]), col(a0[l][perm_j]),
                col(k_k[l][perm_j]), col(k_a[l][perm_j]))
        outs = _prompt_front(x, norm_g[l], w_nat, w_t, cols, w_w2[l][:, perm_j].T, w_a2[l][:, perm_j].T,
                             memk[l], memv[l], _block_diag(pool_w[l]), pool_scale[l], tt, width, heads)
        d_t, b_t, k_t, r_t, kk_t, v_t, gate_t, px, ptail, shift_t = outs
        pool_p.append(ptail[:, POOL_HALO - POOL_BUF:])
        shift_p.append(shift_t[:, :, LANES - 1][:, rw_inv])
        gn = lambda p: p[perm_i].reshape(HEAD_DIM, heads, 1)
        y_t, s_c = _rwkv_scan(d_t, b_t, k_t, r_t, kk_t, v_t, col(r_k[l].reshape(-1)[perm_j]),
                              gn(ln_x_g[l]), gn(ln_x_b[l]), heads)
        wkv_p.append(s_c.reshape(HEAD_DIM, half, 2, b, heads).transpose(3, 4, 2, 1, 0)
                     .reshape(b, heads, HEAD_DIM, HEAD_DIM))
        w_px = jnp.concatenate([w_out_b[l][0:pw], w_out_b[l][pw + width:]], axis=0)
        w_rw = w_out_b[l][pw:pw + width][perm_i]
        x = _prompt_back(x, px, y_t, gate_t, w_px, w_rw, final_norm_g, tt, l == depth - 1)
    y_prompt = x

    xs = x_sample.reshape(db, d)
    pool_s, shift_s, wkv_s = [], [], []
    for l in range(depth):
        pv, pg, rw, rg, q, xg = _project(xs, norm_g[l], w_in_b[l], splits, _pick_tile(db, 128))
        pool_s.append(jnp.concatenate([state_pool[l][:, 1:], pv[:, None, :]], axis=1))
        shift_s.append(rw)
        vecs = (pool_scale[l], shift_mu[l], w0[l], a0[l], k_k[l], k_a[l], r_k[l], ln_x_g[l], ln_x_b[l])
        mixed, s_new = _sample_mix(
            pv, pg, rw, rg, q, xg, state_pool[l].transpose(1, 0, 2), state_shift[l], state_wkv[l],
            cache_mem_k[l].reshape(db, n_mem, xw), cache_mem_v[l].reshape(db, n_mem, xw),
            _block_diag(pool_w[l]), vecs, w_w2[l], w_a2[l], bt, start_pos)
        wkv_s.append(s_new)
        xs = _out_project(xs, mixed, w_out_b[l], final_norm_g, l == depth - 1)
    y_sample = xs.reshape(db, 1, d)

    kv_shape = (depth, b, n_mem, XA_HEADS, xw // XA_HEADS)
    return (y_prompt, y_sample, jnp.stack(pool_p), jnp.stack(shift_p), jnp.stack(wkv_p),
            jnp.stack(memk).reshape(kv_shape), jnp.stack(memv).reshape(kv_shape),
            jnp.stack(pool_s), jnp.stack(shift_s), jnp.stack(wkv_s))
```

```python
import functools

import numpy as np
import jax
import jax.numpy as jnp
from jax import lax
from jax.experimental import pallas as pl
from jax.experimental.pallas import tpu as pltpu

F32 = jnp.float32
BF16 = jnp.bfloat16
HIGHEST = lax.Precision.HIGHEST

HEAD_DIM = 64
POOL_WINDOWS = (2, 4, 8, 16)
POOL_BUF = max(POOL_WINDOWS) - 1
POOL_HALO = 16
XA_HEADS = 4
DECAY_LORA = 64
ICLR_LORA = 64
EPS = 1e-6
GN_EPS = HEAD_DIM * 1e-5
LANES = 128
SUBLANES = 8
ROW_PITCH = 40
VMEM_LIMIT = 48 * 1024 * 1024
SCAN_VMEM_LIMIT = 60 * 1024 * 1024


def _cparams(sem, limit=VMEM_LIMIT):
    return pltpu.CompilerParams(dimension_semantics=sem, vmem_limit_bytes=limit)


def _silu(x):
    return x / (1.0 + jnp.exp(-x))


def _sigmoid(x):
    return 1.0 / (1.0 + jnp.exp(-x))


def _softplus(x):
    return jnp.maximum(x, 0.0) + jnp.log1p(jnp.exp(-jnp.abs(x)))


def _split_bf16(x):
    hi = x.astype(BF16)
    return hi, (x - hi.astype(F32)).astype(BF16)


def _dot3(a, b):
    a_hi, a_lo = _split_bf16(a)
    b_hi, b_lo = _split_bf16(b)
    dot = functools.partial(jnp.dot, preferred_element_type=F32)
    return dot(a_hi, b_hi) + (dot(a_hi, b_lo) + dot(a_lo, b_hi))


def _lane_sum_bcast(x):
    b, r, n = x.shape
    hi, lo = _split_bf16(x.reshape(b * r, n))
    ones = jnp.ones((n, n), BF16)
    dot = functools.partial(jnp.dot, preferred_element_type=F32)
    return (dot(hi, ones) + dot(lo, ones)).reshape(b, r, n)


def _rms(x, g):
    return x * lax.rsqrt(jnp.mean(x * x, axis=-1, keepdims=True) + EPS) * g


def _proj_kernel(x_ref, g_ref, w_ref, *out_refs, splits):
    xn = _rms(x_ref[...], g_ref[...]).astype(BF16)
    off = 0
    for o_ref, n in zip(out_refs, splits):
        o_ref[...] = jnp.dot(xn, w_ref[:, off:off + n], preferred_element_type=F32)
        off += n


def _project(x2d, g, w_bf16, splits, tm):
    m, d = x2d.shape
    n = w_bf16.shape[1]
    return pl.pallas_call(
        functools.partial(_proj_kernel, splits=splits),
        grid=(m // tm,),
        in_specs=[pl.BlockSpec((tm, d), lambda i: (i, 0)),
                  pl.BlockSpec((1, d), lambda i: (0, 0)),
                  pl.BlockSpec((d, n), lambda i: (0, 0))],
        out_specs=[pl.BlockSpec((tm, s), lambda i: (i, 0)) for s in splits],
        out_shape=[jax.ShapeDtypeStruct((m, s), F32) for s in splits],
        compiler_params=_cparams(("arbitrary",)),
        name="proj",
    )(x2d, g.reshape(1, d), w_bf16)


def _pool_select(lane_group, vals):
    out = vals[-1]
    for gi in range(len(vals) - 2, -1, -1):
        out = jnp.where(lane_group == gi, vals[gi], out)
    return out


def _cross_attend_rows(qm, mk, mv):
    hd = mk.shape[1] // XA_HEADS
    s = lax.dot_general(qm, mk, (((1,), (1,)), ((), ())), preferred_element_type=F32) * (hd ** -0.5)
    p = jnp.exp(s - jnp.max(s, axis=-1, keepdims=True))
    p = p / jnp.sum(p, axis=-1, keepdims=True)
    return jnp.dot(p.astype(mv.dtype), mv, preferred_element_type=F32)


def _front_kernel(x_ref, g_ref, wn_ref, wt_ref, mu_ref, w0_ref, ww2_ref, a0_ref, wa2_ref, kk_ref, ka_ref,
                  mk_ref, mv_ref, wbd_ref, ps_ref,
                  d_o, b_o, k_o, r_o, kkn_o, v_o, gate_o, px_o, ptail_o, shift_o,
                  ext, carry, *, tt, width, heads):
    ti = pl.program_id(1)

    @pl.when(ti == 0)
    def _():
        ext[0:POOL_HALO, :] = jnp.zeros((POOL_HALO, ext.shape[1]), F32)
        carry[...] = jnp.zeros_like(carry)

    xn = _rms(x_ref[...], g_ref[...]).astype(BF16)
    nat = jnp.dot(xn, wn_ref[...], preferred_element_type=F32)
    cm = lax.dot_general(wt_ref[...], xn, (((1,), (1,)), ((), ())), preferred_element_type=F32)
    sw = mu_ref.shape[0]
    rwt = cm[0:sw]
    gate_o[...] = _silu(cm[sw:])

    prev = pltpu.roll(jnp.concatenate([carry[...], rwt], axis=1), 1, 1)[:, LANES:]
    carry[...] = rwt[:, tt - LANES:]
    shift_o[...] = rwt[:, tt - LANES:]
    rws = rwt + (prev - rwt) * mu_ref[...]
    r = rws[0:width]
    k = rws[width:2 * width]
    v = rws[2 * width:3 * width]
    wd = rws[3 * width:3 * width + DECAY_LORA]
    ad = rws[3 * width + DECAY_LORA:]
    wlin = w0_ref[...] + _dot3(ww2_ref[...], jnp.tanh(wd))
    decay = jnp.exp(-jnp.exp(-_softplus(-wlin) - 0.5))
    a = _sigmoid(a0_ref[...] + _dot3(wa2_ref[...], ad))
    kk3 = (k * kk_ref[...]).reshape(HEAD_DIM, heads, tt)
    nrm = jnp.maximum(jnp.sqrt(jnp.sum(kk3 * kk3, axis=0, keepdims=True)), 1e-12)
    kkn = (kk3 / nrm).reshape(width, tt)
    d_o[...] = decay
    b_o[...] = kkn * a
    k_o[...] = k * (1.0 + (a - 1.0) * ka_ref[...])
    r_o[...] = r
    kkn_o[...] = kkn
    v_o[...] = v

    pw = ps_ref.shape[1]
    xw = mk_ref.shape[1]
    pv = nat[:, 0:pw]
    pg = nat[:, pw:2 * pw]
    q = nat[:, 2 * pw:2 * pw + xw]
    xg = nat[:, 2 * pw + xw:]

    ext[POOL_HALO:POOL_HALO + tt, :] = pv
    sums = []
    acc = pv
    for back in range(1, max(POOL_WINDOWS)):
        acc = acc + ext[POOL_HALO - back:POOL_HALO - back + tt, :]
        if back + 1 in POOL_WINDOWS:
            sums.append(acc)
    tail = ext[tt:tt + POOL_HALO, :]
    ext[0:POOL_HALO, :] = tail
    ptail_o[...] = tail
    group = lax.broadcasted_iota(jnp.int32, (tt, pw), 1) // (pw // len(POOL_WINDOWS))
    pos = ti * tt + lax.broadcasted_iota(jnp.int32, (tt, pw), 0)
    win = _pool_select(group, [jnp.full((tt, pw), w, jnp.int32) for w in POOL_WINDOWS])
    cnt = jnp.minimum(pos + 1, win).astype(F32)
    pooled = _pool_select(group, sums) / cnt - pv
    pool_y = _dot3(pooled, wbd_ref[...])
    px_o[:, 0:pw] = (pool_y * ps_ref[...] * _silu(pg)).astype(BF16)

    mk = mk_ref[...].astype(BF16)
    mv = mv_ref[...].astype(BF16)
    head = lax.broadcasted_iota(jnp.int32, (tt, xw), 1) // (xw // XA_HEADS)
    xa = jnp.zeros((tt, xw), F32)
    for h in range(XA_HEADS):
        qm = jnp.where(head == h, q, 0.0).astype(BF16)
        xa = jnp.where(head == h, _cross_attend_rows(qm, mk, mv), xa)
    px_o[:, pw:] = (xa * _silu(xg)).astype(BF16)


def _prompt_front(x, g, w_nat, w_t, cols, ww2_t, wa2_t, mk, mv, wbd, pscale, tt, width, heads):
    b, t, d = x.shape
    mu_c, w0_c, a0_c, kk_c, ka_c = cols
    sw = mu_c.shape[0]
    pw = pscale.shape[-1]
    xw = mk.shape[2]
    g = g.reshape(1, d)
    pscale = pscale.reshape(1, pw)
    const = lambda a: pl.BlockSpec(a.shape, lambda bi, ti: (0,) * a.ndim)
    cmaj = pl.BlockSpec((None, width, tt), lambda bi, ti: (bi, 0, ti))
    perb = lambda s: pl.BlockSpec((None,) + s, lambda bi, ti: (bi, 0, 0))
    return pl.pallas_call(
        functools.partial(_front_kernel, tt=tt, width=width, heads=heads),
        grid=(b, t // tt),
        in_specs=[pl.BlockSpec((None, tt, d), lambda bi, ti: (bi, ti, 0)), const(g), const(w_nat), const(w_t),
                  const(mu_c), const(w0_c), const(ww2_t), const(a0_c), const(wa2_t), const(kk_c), const(ka_c),
                  perb(mk.shape[1:]), perb(mv.shape[1:]), const(wbd), const(pscale)],
        out_specs=[cmaj] * 7 + [pl.BlockSpec((None, tt, pw + xw), lambda bi, ti: (bi, ti, 0)),
                                perb((POOL_HALO, pw)), perb((sw, LANES))],
        out_shape=[jax.ShapeDtypeStruct((b, width, t), F32)] * 7
        + [jax.ShapeDtypeStruct((b, t, pw + xw), BF16),
           jax.ShapeDtypeStruct((b, POOL_HALO, pw), F32),
           jax.ShapeDtypeStruct((b, sw, LANES), F32)],
        scratch_shapes=[pltpu.VMEM((tt + POOL_HALO, pw), F32), pltpu.VMEM((sw, LANES), F32)],
        compiler_params=_cparams(("arbitrary", "arbitrary")),
        name="front",
    )(x, g, w_nat, w_t, mu_c, w0_c, ww2_t, a0_c, wa2_t, kk_c, ka_c, mk, mv, wbd, pscale)


def _scan_kernel(d_ref, b_ref, k_ref, r_ref, kk_ref, v_ref, rk_ref, g_ref, be_ref,
                 y_ref, s_ref, wj, bj, kj, rj, kkj, vy, u_scr, *, n, heads, nb):
    half = n // 2
    chains = nb * heads

    @pl.when(pl.program_id(0) == 0)
    def _():
        s_ref[...] = jnp.zeros_like(s_ref)

    def relayout_j(j, carry):
        row0 = pl.multiple_of(j * heads, SUBLANES)
        for src, dst in ((d_ref, wj), (b_ref, bj), (k_ref, kj), (r_ref, rj), (kk_ref, kkj)):
            m = src[:, pl.ds(row0, heads), :].reshape(chains, LANES)
            dst[j] = jnp.concatenate([m, m], axis=0).T
        return carry

    lax.fori_loop(0, n, relayout_j, 0)
    for il in range(half):
        lo = v_ref[:, (2 * il) * heads:(2 * il + 1) * heads, :].reshape(chains, LANES)
        hi = v_ref[:, (2 * il + 1) * heads:(2 * il + 2) * heads, :].reshape(chains, LANES)
        vy[pl.ds(il, LANES, stride=ROW_PITCH), :] = jnp.concatenate([lo, hi], axis=0).T

    acc = jnp.zeros((half, LANES), F32)
    for j in range(n):
        acc = acc + s_ref[j] * kkj[j, 0:1, :]
    u_scr[...] = acc

    def step(t, carry):
        tn = jnp.minimum(t + 1, LANES - 1)
        row = pl.multiple_of(t * ROW_PITCH, SUBLANES)
        u = u_scr[...]
        v = vy[pl.ds(row, half), :]
        y = jnp.zeros((half, LANES), F32)
        nacc = jnp.zeros((half, LANES), F32)
        for j in range(n):
            s = (s_ref[j] * wj[j, pl.ds(t, 1), :] - u * bj[j, pl.ds(t, 1), :]
                 + v * kj[j, pl.ds(t, 1), :])
            s_ref[j] = s
            y = y + s * rj[j, pl.ds(t, 1), :]
            nacc = nacc + s * kkj[j, pl.ds(tn, 1), :]
        u_scr[...] = nacc
        vy[pl.ds(row, half), :] = y
        return carry

    lax.fori_loop(0, LANES, step, 0)

    for il in range(half):
        yt = vy[pl.ds(il, LANES, stride=ROW_PITCH), :].T
        for ih in range(2):
            for bi in range(nb):
                src0 = (ih * nb + bi) * heads
                dst0 = (2 * il + ih) * heads
                y_ref[bi, dst0:dst0 + heads, :] = yt[src0:src0 + heads, :]
    for bi in range(nb):
        y3 = y_ref[bi].reshape(n, heads, LANES)
        d = y3 - jnp.sum(y3, axis=0, keepdims=True) * (1.0 / n)
        var = jnp.sum(d * d, axis=0, keepdims=True) * (1.0 / n)
        bonus = jnp.sum((r_ref[bi] * k_ref[bi] * rk_ref[...]).reshape(n, heads, LANES), axis=0, keepdims=True)
        out = d * lax.rsqrt(var + GN_EPS) * g_ref[...] + be_ref[...] + bonus * v_ref[bi].reshape(n, heads, LANES)
        y_ref[bi] = out.reshape(n * heads, LANES)


def _rwkv_scan(d_t, b_t, k_t, r_t, kk_t, v_t, rk_c, g_c, be_c, heads):
    nb, width, t = d_t.shape
    n = width // heads
    half = n // 2
    tile = pl.BlockSpec((nb, width, LANES), lambda ti: (0, 0, ti))
    const = lambda a: pl.BlockSpec(a.shape, lambda ti: (0,) * a.ndim)
    jscr = pltpu.VMEM((n, LANES, LANES), F32)
    iscr = pltpu.VMEM((LANES * ROW_PITCH, LANES), F32)
    return pl.pallas_call(
        functools.partial(_scan_kernel, n=n, heads=heads, nb=nb),
        grid=(t // LANES,),
        in_specs=[tile] * 6 + [const(rk_c), const(g_c), const(be_c)],
        out_specs=[tile,
                   pl.BlockSpec((n, half, LANES), lambda ti: (0, 0, 0))],
        out_shape=[jax.ShapeDtypeStruct((nb, width, t), F32),
                   jax.ShapeDtypeStruct((n, half, LANES), F32)],
        scratch_shapes=[jscr] * 5 + [iscr, pltpu.VMEM((half, LANES), F32)],
        compiler_params=_cparams(("arbitrary",), SCAN_VMEM_LIMIT),
        name="rwkv_scan",
    )(d_t, b_t, k_t, r_t, kk_t, v_t, rk_c, g_c, be_c)


def _back_kernel(x_ref, px_ref, yt_ref, gate_ref, wpx_ref, wrw_ref, fg_ref, o_ref, *, final):
    yg = (yt_ref[...] * gate_ref[...]).T.astype(BF16)
    out = x_ref[...] + jnp.dot(px_ref[...], wpx_ref[...], preferred_element_type=F32)
    out = out + jnp.dot(yg, wrw_ref[...], preferred_element_type=F32)
    if final:
        out = _rms(out, fg_ref[...])
    o_ref[...] = out


def _prompt_back(x, px, y_t, gate_t, w_px, w_rw, fin_g, tt, final):
    b, t, d = x.shape
    width = y_t.shape[1]
    fin_g = fin_g.reshape(1, d)
    const = lambda a: pl.BlockSpec(a.shape, lambda bi, ti: (0,) * a.ndim)
    tile = lambda c: pl.BlockSpec((None, tt, c), lambda bi, ti: (bi, ti, 0))
    cmaj = pl.BlockSpec((None, width, tt), lambda bi, ti: (bi, 0, ti))
    return pl.pallas_call(
        functools.partial(_back_kernel, final=final),
        grid=(b, t // tt),
        in_specs=[tile(d), tile(px.shape[2]), cmaj, cmaj, const(w_px), const(w_rw), const(fin_g)],
        out_specs=tile(d),
        out_shape=jax.ShapeDtypeStruct((b, t, d), F32),
        compiler_params=_cparams(("arbitrary", "arbitrary")),
        name="back",
    )(x, px, y_t, gate_t, w_px, w_rw, fin_g)


def _rwkv_vectors(rws, w0, w_w2, a0, w_a2, k_k, k_a, width):
    r = rws[:, :width]
    k = rws[:, width:2 * width]
    v = rws[:, 2 * width:3 * width]
    wd = rws[:, 3 * width:3 * width + DECAY_LORA]
    ad = rws[:, 3 * width + DECAY_LORA:]
    wlin = w0 + jnp.dot(jnp.tanh(wd), w_w2, precision=HIGHEST, preferred_element_type=F32)
    decay = jnp.exp(-jnp.exp(-_softplus(-wlin) - 0.5))
    a = _sigmoid(a0 + jnp.dot(ad, w_a2, precision=HIGHEST, preferred_element_type=F32))
    kk_raw = k * k_k
    k = k * (1.0 + (a - 1.0) * k_a)
    return r, k, v, decay, kk_raw, a


def _sample_kernel(pv_ref, pg_ref, rw_ref, rg_ref, q_ref, xg_ref, pbuf_ref, sprev_ref, s_ref,
                   ck_ref, cv_ref, wbd_ref, ps_ref, mu_ref, w0_ref, ww2_ref, a0_ref, wa2_ref,
                   kk_ref, ka_ref, rk_ref, g_ref, be_ref,
                   mixed_ref, so_ref, y_scr, xa_scr, *, bt, start_pos):
    n = HEAD_DIM
    pv = pv_ref[...]
    pw = pv.shape[1]
    sums = []
    acc = pv
    for back in range(1, max(POOL_WINDOWS)):
        acc = acc + pbuf_ref[POOL_BUF - back]
        if back + 1 in POOL_WINDOWS:
            sums.append(acc)
    group = lax.broadcasted_iota(jnp.int32, (bt, pw), 1) // (pw // len(POOL_WINDOWS))
    cnts = [jnp.full((bt, pw), float(min(start_pos + 1, w)), F32) for w in POOL_WINDOWS]
    pooled = _pool_select(group, sums) / _pool_select(group, cnts) - pv
    pool_y = jnp.dot(pooled, wbd_ref[...], precision=HIGHEST, preferred_element_type=F32)
    pool_y = pool_y * ps_ref[...] * _silu(pg_ref[...])

    rw = rw_ref[...]
    rws = rw + (sprev_ref[...] - rw) * mu_ref[...]
    width = w0_ref.shape[1]
    heads = width // n
    r, k, v, decay, kk_raw, a = _rwkv_vectors(
        rws, w0_ref[...], ww2_ref[...], a0_ref[...], wa2_ref[...], kk_ref[...], ka_ref[...], width)
    eye = (lax.broadcasted_iota(jnp.int32, (n, n), 0) == lax.broadcasted_iota(jnp.int32, (n, n), 1))[None]
    for h in range(heads):
        sl = slice(h * n, (h + 1) * n)
        r_h, k_h, v_h = r[:, sl], k[:, sl], v[:, sl]
        kk_h = kk_raw[:, sl]
        kk_h = kk_h / jnp.maximum(jnp.sqrt(jnp.sum(kk_h * kk_h, axis=-1, keepdims=True)), 1e-12)
        b_h = kk_h * a[:, sl]
        s = s_ref[:, h]
        s_kk = _lane_sum_bcast(s * kk_h[:, None, :])
        v_col = jnp.sum(jnp.where(eye, v_h[:, None, :], 0.0), axis=-1, keepdims=True)
        s = s * decay[:, sl][:, None, :] - s_kk * b_h[:, None, :] + v_col * k_h[:, None, :]
        so_ref[:, h] = s
        y_col = _lane_sum_bcast(s * r_h[:, None, :])
        y = jnp.sum(jnp.where(eye, y_col, 0.0), axis=1)
        d = y - jnp.mean(y, axis=-1, keepdims=True)
        var = jnp.mean(d * d, axis=-1, keepdims=True)
        y = d * lax.rsqrt(var + GN_EPS) * g_ref[:, sl] + be_ref[:, sl]
        bonus = jnp.sum(r_h * k_h * rk_ref[:, sl], axis=-1, keepdims=True)
        y_scr[:, sl] = y + bonus * v_h
    rwkv_y = y_scr[...] * _silu(rg_ref[...])

    q = q_ref[...]
    xw = q.shape[1]
    row_h = lax.broadcasted_iota(jnp.int32, (SUBLANES, xw), 0)
    lane_h = lax.broadcasted_iota(jnp.int32, (SUBLANES, xw), 1) // (xw // XA_HEADS)
    for bi in range(bt):
        qm = jnp.where(row_h == lane_h, q[bi:bi + 1, :], 0.0)
        o8 = _cross_attend_rows(qm, ck_ref[bi], cv_ref[bi])
        xa_scr[bi:bi + 1, :] = jnp.sum(jnp.where(row_h == lane_h, o8, 0.0), axis=0, keepdims=True)
    xa_y = xa_scr[...] * _silu(xg_ref[...])

    mixed_ref[:, 0:pw] = pool_y
    mixed_ref[:, pw:pw + width] = rwkv_y
    mixed_ref[:, pw + width:] = xa_y


def _sample_mix(pv, pg, rw, rg, q, xg, pbuf_t, sprev, s0, ck, cv, wbd, vecs, ww2, wa2, bt, start_pos):
    b = pv.shape[0]
    row = lambda c: pl.BlockSpec((bt, c), lambda i: (i, 0))
    const = lambda a: pl.BlockSpec(a.shape, lambda i: (0,) * a.ndim)
    lead = lambda a: pl.BlockSpec((bt,) + a.shape[1:], lambda i: (i,) + (0,) * (a.ndim - 1))
    pscale, mu, w0, a0, k_k, k_a, r_k, ln_g, ln_b = [a.reshape(1, -1) for a in vecs]
    mixw = pv.shape[1] + w0.shape[1] + q.shape[1]
    return pl.pallas_call(
        functools.partial(_sample_kernel, bt=bt, start_pos=start_pos),
        grid=(b // bt,),
        in_specs=[row(pv.shape[1]), row(pg.shape[1]), row(rw.shape[1]), row(rg.shape[1]),
                  row(q.shape[1]), row(xg.shape[1]),
                  pl.BlockSpec((POOL_BUF, bt, pv.shape[1]), lambda i: (0, i, 0)),
                  row(sprev.shape[1]), lead(s0), lead(ck), lead(cv),
                  const(wbd), const(pscale), const(mu), const(w0), const(ww2), const(a0), const(wa2),
                  const(k_k), const(k_a), const(r_k), const(ln_g), const(ln_b)],
        out_specs=[row(mixw), lead(s0)],
        out_shape=[jax.ShapeDtypeStruct((b, mixw), F32), jax.ShapeDtypeStruct(s0.shape, F32)],
        scratch_shapes=[pltpu.VMEM((bt, w0.shape[1]), F32), pltpu.VMEM((bt, q.shape[1]), F32)],
        compiler_params=_cparams(("arbitrary",)),
        name="sample_mix",
    )(pv, pg, rw, rg, q, xg, pbuf_t, sprev, s0, ck, cv, wbd, pscale, mu, w0, ww2, a0, wa2,
      k_k, k_a, r_k, ln_g, ln_b)


def _out_kernel(x_ref, m_ref, wo_ref, fg_ref, o_ref, *, final):
    out = x_ref[...] + jnp.dot(m_ref[...].astype(BF16), wo_ref[...], preferred_element_type=F32)
    if final:
        out = _rms(out, fg_ref[...])
    o_ref[...] = out


def _out_project(x2d, mixed, wo_bf16, fin_g, final):
    m, d = x2d.shape
    full = lambda a: pl.BlockSpec(a.shape, lambda i: (0,) * a.ndim)
    fin_g = fin_g.reshape(1, -1)
    return pl.pallas_call(
        functools.partial(_out_kernel, final=final),
        grid=(1,),
        in_specs=[full(x2d), full(mixed), full(wo_bf16), full(fin_g)],
        out_specs=full(x2d),
        out_shape=jax.ShapeDtypeStruct((m, d), F32),
        compiler_params=_cparams(("arbitrary",)),
        name="out_proj",
    )(x2d, mixed, wo_bf16, fin_g)


def _block_diag(pool_w):
    g_n, g, _ = pool_w.shape
    eye = jnp.eye(g_n, dtype=pool_w.dtype)
    return (eye[:, None, :, None] * pool_w[:, :, None, :]).reshape(g_n * g, g_n * g)


def _pick_tile(n, pref):
    t = min(n, pref)
    while n % t:
        t //= 2
    return t


def _channel_orders(heads):
    n = HEAD_DIM
    j = np.arange(n)[:, None]
    h = np.arange(heads)[None, :]
    perm_j = (h * n + j).reshape(-1)
    il = np.arange(n // 2)[:, None, None]
    ih = np.arange(2)[None, :, None]
    h3 = np.arange(heads)[None, None, :]
    perm_i = (h3 * n + ih * (n // 2) + il).reshape(-1)
    return perm_j, perm_i


def kernel(x_prompt, x_sample, mem_prompt, state_pool, state_shift, state_wkv, cache_mem_k, cache_mem_v, norm_g, w_in, w_out, pool_w, pool_scale, shift_mu, w0, w_w2, a0, w_a2, k_k, k_a, r_k, ln_x_g, ln_x_b, mem_norm_g, w_kv, final_norm_g):
    depth = w_in.shape[0]
    b, t, d = x_prompt.shape
    db = x_sample.shape[0]
    n_mem = mem_prompt.shape[1]
    pw = pool_scale.shape[1]
    width = w0.shape[1]
    heads = width // HEAD_DIM
    sw = shift_mu.shape[1]
    xw = w_kv.shape[2] // 2
    half = HEAD_DIM // 2
    assert 2 * b * heads == LANES, "the recurrence kernel places the (batch, head) chains on the lanes twice"
    assert t % LANES == 0
    splits = (pw, pw, sw, width, xw, xw)
    start_pos = 16384

    w_in_b = w_in.astype(BF16)
    w_out_b = w_out.astype(BF16)
    w_kv_b = w_kv.astype(BF16)

    perm_j, perm_i = _channel_orders(heads)
    rw_perm = np.concatenate([perm_j, width + perm_j, 2 * width + perm_i, 3 * width + np.arange(sw - 3 * width)])
    rw_inv = np.argsort(rw_perm)
    rw0 = 2 * pw
    rg0 = rw0 + sw
    t_cols = np.concatenate([rw0 + rw_perm, rg0 + perm_i])
    n_cols = np.concatenate([np.arange(0, rw0), np.arange(rg0 + width, rg0 + width + 2 * xw)])
    col = lambda a: a.reshape(-1, 1)

    tt = _pick_tile(t, 512)
    bt = _pick_tile(db, 8)

    mem2d = mem_prompt.reshape(b * n_mem, d)
    memk, memv = [], []
    for l in range(depth):
        mk, mv = _project(mem2d, mem_norm_g[l], w_kv_b[l], (xw, xw), _pick_tile(b * n_mem, 512))
        memk.append(mk.reshape(b, n_mem, xw))
        memv.append(mv.reshape(b, n_mem, xw))

    x = x_prompt
    pool_p, shift_p, wkv_p = [], [], []
    for l in range(depth):
        w_t = w_in_b[l][:, t_cols].T
        w_nat = w_in_b[l][:, n_cols]
        cols = (col(shift_mu[l][rw_perm]), col(w0[l][perm_j]), col(a0[l][perm_j]),
                col(k_k[l][perm_j]), col(k_a[l][perm_j]))
        outs = _prompt_front(x, norm_g[l], w_nat, w_t, cols, w_w2[l][:, perm_j].T, w_a2[l][:, perm_j].T,
                             memk[l], memv[l], _block_diag(pool_w[l]), pool_scale[l], tt, width, heads)
        d_t, b_t, k_t, r_t, kk_t, v_t, gate_t, px, ptail, shift_t = outs
        pool_p.append(ptail[:, POOL_HALO - POOL_BUF:])
        shift_p.append(shift_t[:, :, LANES - 1][:, rw_inv])
        gn = lambda p: p[perm_i].reshape(HEAD_DIM, heads, 1)
        y_t, s_c = _rwkv_scan(d_t, b_t, k_t, r_t, kk_t, v_t, col(r_k[l].reshape(-1)[perm_j]),
                              gn(ln_x_g[l]), gn(ln_x_b[l]), heads)
        wkv_p.append(s_c.reshape(HEAD_DIM, half, 2, b, heads).transpose(3, 4, 2, 1, 0)
                     .reshape(b, heads, HEAD_DIM, HEAD_DIM))
        w_px = jnp.concatenate([w_out_b[l][0:pw], w_out_b[l][pw + width:]], axis=0)
        w_rw = w_out_b[l][pw:pw + width][perm_i]
        x = _prompt_back(x, px, y_t, gate_t, w_px, w_rw, final_norm_g, tt, l == depth - 1)
    y_prompt = x

    xs = x_sample.reshape(db, d)
    pool_s, shift_s, wkv_s = [], [], []
    for l in range(depth):
        pv, pg, rw, rg, q, xg = _project(xs, norm_g[l], w_in_b[l], splits, _pick_tile(db, 128))
        pool_s.append(jnp.concatenate([state_pool[l][:, 1:], pv[:, None, :]], axis=1))
        shift_s.append(rw)
        vecs = (pool_scale[l], shift_mu[l], w0[l], a0[l], k_k[l], k_a[l], r_k[l], ln_x_g[l], ln_x_b[l])
        mixed, s_new = _sample_mix(
            pv, pg, rw, rg, q, xg, state_pool[l].transpose(1, 0, 2), state_shift[l], state_wkv[l],
            cache_mem_k[l].reshape(db, n_mem, xw), cache_mem_v[l].reshape(db, n_mem, xw),
            _block_diag(pool_w[l]), vecs, w_w2[l], w_a2[l], bt, start_pos)
        wkv_s.append(s_new)
        xs = _out_project(xs, mixed, w_out_b[l], final_norm_g, l == depth - 1)
    y_sample = xs.reshape(db, 1, d)

    kv_shape = (depth, b, n_mem, XA_HEADS, xw // XA_HEADS)
    return (y_prompt, y_sample, jnp.stack(pool_p), jnp.stack(shift_p), jnp.stack(wkv_p),
            jnp.stack(memk).reshape(kv_shape), jnp.stack(memv).reshape(kv_shape),
            jnp.stack(pool_s), jnp.stack(shift_s), jnp.stack(wkv_s))
```
